```python
import jax, jax.numpy as jnp
from jax import lax
import numpy as np

D_MODEL = 1024
BATCH = 2
SEQ = 8192
DEPTH = 1
DEC_BATCH = 128
DEC_SEQ = 8
PAST_LEN = 8192
PAGE_SIZE = 128

HEAD_DIM = 64
D_MIX = D_MODEL
D_SB = D_MIX // 2
D_FOX = D_MIX - D_SB
H_SB = D_SB // HEAD_DIM
H_FOX = D_FOX // HEAD_DIM
PLE_DIM = 256
Q_BLOCK = 128
EPS = 1e-6
SCALE = HEAD_DIM ** -0.5
N_IN = 4 * D_SB + 4 * D_FOX + H_FOX
SPLIT_POINTS = (D_SB, 2 * D_SB, 3 * D_SB, 4 * D_SB,
                4 * D_SB + D_FOX, 4 * D_SB + 2 * D_FOX, 4 * D_SB + 3 * D_FOX, 4 * D_SB + 4 * D_FOX)

kernel_name = "hymba_stickbreak_fox_decoder_step"


def rmsnorm(x, g):
    xf = x.astype(jnp.float32)
    y = xf * lax.rsqrt(jnp.mean(xf * xf, axis=-1, keepdims=True) + EPS)
    return (y * g.astype(jnp.float32)).astype(x.dtype)


def head_rmsnorm(o, g):
    of = o.astype(jnp.float32)
    y = of * lax.rsqrt(jnp.mean(of * of, axis=-1, keepdims=True) + EPS)
    y = y * g.astype(jnp.float32).reshape(o.shape[-2:])
    return y.reshape(o.shape[:-2] + (o.shape[-2] * o.shape[-1],)).astype(o.dtype)


def project(h, g_norm, w_in, b_f):
    u = rmsnorm(h, g_norm)
    z = u @ w_in
    qs, ks, vs, zs, qf, kf, vf, zf, ff = jnp.split(z, SPLIT_POINTS, axis=-1)
    heads = lambda t, n: t.reshape(t.shape[:-1] + (n, HEAD_DIM))
    logf = jax.nn.log_sigmoid(ff + b_f)
    return (heads(qs, H_SB), heads(ks, H_SB), heads(vs, H_SB),
            heads(qf, H_FOX), heads(kf, H_FOX), heads(vf, H_FOX), zs, zf, logf)


def stick_breaking_block(q, k, v, q_pos, k_pos):
    z = jnp.einsum('bqhd,bkhd->bhqk', q.astype(jnp.float32), k.astype(jnp.float32)) * SCALE
    mask = k_pos[None, :] < q_pos[:, None]
    log_keep = jnp.where(mask, jax.nn.log_sigmoid(-z), 0.0)
    after = lax.cumsum(log_keep, axis=3, reverse=True) - log_keep
    w = jnp.where(mask, jnp.exp(jax.nn.log_sigmoid(z) + after), 0.0)
    o = jnp.einsum('bhqk,bkhd->bqhd', w, v.astype(jnp.float32))
    return o.astype(q.dtype)


def forgetting_block(q, k, v, cq, ck, q_pos, k_pos):
    s = jnp.einsum('bqhd,bkhd->bhqk', q.astype(jnp.float32), k.astype(jnp.float32)) * SCALE
    s = s + jnp.transpose(cq, (0, 2, 1))[..., None] - jnp.transpose(ck, (0, 2, 1))[:, :, None, :]
    mask = k_pos[None, :] <= q_pos[:, None]
    p = jax.nn.softmax(jnp.where(mask, s, -jnp.inf), axis=-1)
    o = jnp.einsum('bhqk,bkhd->bqhd', p, v.astype(jnp.float32))
    return o.astype(q.dtype)


def prompt_mixers(q_sb, k_sb, v_sb, q_fox, k_fox, v_fox, logf):
    b, s = q_sb.shape[0], q_sb.shape[1]
    nb = s // Q_BLOCK
    k_pos = jnp.arange(s)
    c = jnp.cumsum(logf.astype(jnp.float32), axis=1)

    def to_blocks(t):
        return jnp.moveaxis(t.reshape((b, nb, Q_BLOCK) + t.shape[2:]), 1, 0)

    def body(args):
        qs_blk, qf_blk, c_blk, start = args
        q_pos = start + jnp.arange(Q_BLOCK)
        o_s = stick_breaking_block(qs_blk, k_sb, v_sb, q_pos, k_pos)
        o_f = forgetting_block(qf_blk, k_fox, v_fox, c_blk, c, q_pos, k_pos)
        return o_s, o_f

    o_s, o_f = lax.map(body, (to_blocks(q_sb), to_blocks(q_fox), to_blocks(c),
                              jnp.arange(nb, dtype=jnp.int32) * Q_BLOCK))
    from_blocks = lambda t: jnp.moveaxis(t, 0, 1).reshape((b, s) + t.shape[3:])
    return from_blocks(o_s), from_blocks(o_f)


def sample_mixers(q_sb, k_sb, v_sb, q_fox, k_fox, v_fox, logf,
                  kv_sb_pool, kv_fox_pool, logf_pool, page_table):
    n_pages = page_table.shape[1]
    past = n_pages * kv_sb_pool.shape[1]
    t_new = q_sb.shape[1]
    q_pos = past + jnp.arange(t_new)
    k_pos = jnp.arange(past + t_new)

    def body(args):
        qs, ks, vs, qf, kf, vf, lf, pages = args
        kv_s = kv_sb_pool[pages].reshape((past,) + kv_sb_pool.shape[2:])
        ks_all = jnp.concatenate([kv_s[:, 0], ks], axis=0)
        vs_all = jnp.concatenate([kv_s[:, 1], vs], axis=0)
        o_s = stick_breaking_block(qs[None], ks_all[None], vs_all[None], q_pos, k_pos)[0]
        kv_f = kv_fox_pool[pages].reshape((past,) + kv_fox_pool.shape[2:])
        kf_all = jnp.concatenate([kv_f[:, 0], kf], axis=0)
        vf_all = jnp.concatenate([kv_f[:, 1], vf], axis=0)
        lf_all = jnp.concatenate([logf_pool[pages].reshape((past,) + logf_pool.shape[2:]), lf], axis=0)
        c = jnp.cumsum(lf_all.astype(jnp.float32), axis=0)
        o_f = forgetting_block(qf[None], kf_all[None], vf_all[None], c[past:][None], c[None],
                               q_pos, k_pos)[0]
        return o_s, o_f

    return lax.map(body, (q_sb, k_sb, v_sb, q_fox, k_fox, v_fox, logf, page_table))


def merge(h, o_sb, o_fox, z_sb, z_fox, g_out_sb, g_out_fox, w_out, p, w_ple, g_ple, w_ple_gate):
    a = head_rmsnorm(o_sb, g_out_sb) * jax.nn.silu(z_sb)
    b = head_rmsnorm(o_fox, g_out_fox) * jax.nn.silu(z_fox)
    h = h + jnp.concatenate([a, b], axis=-1) @ w_out
    gate = jax.nn.sigmoid(rmsnorm(h, g_ple) @ w_ple_gate)
    return h + (p @ w_ple) * gate


def setup_inputs(seed: int = 0) -> dict:
    key = jax.random.key(seed)
    ks = jax.random.split(key, 20)
    n_pages = PAST_LEN // PAGE_SIZE
    n_used = DEC_BATCH * n_pages
    n_pool = n_used + max(1, n_used // 4)
    f32 = jnp.float32
    nrm = lambda k, shape, s: jax.random.normal(k, shape, f32) * s
    page_table = jax.random.permutation(ks[0], n_pool)[:n_used].reshape(DEC_BATCH, n_pages).astype(jnp.int32)
    return {
        "x_prompt": nrm(ks[1], (BATCH, SEQ, D_MODEL), 1.0),
        "x_sample": nrm(ks[2], (DEC_BATCH, DEC_SEQ, D_MODEL), 1.0),
        "cache_sb_kv": nrm(ks[3], (DEPTH, n_pool, PAGE_SIZE, 2, H_SB, HEAD_DIM), 1.0),
        "cache_fox_kv": nrm(ks[4], (DEPTH, n_pool, PAGE_SIZE, 2, H_FOX, HEAD_DIM), 1.0),
        "cache_fox_logf": jax.nn.log_sigmoid(2.0 + nrm(ks[5], (DEPTH, n_pool, PAGE_SIZE, H_FOX), 1.0)),
        "page_table": page_table,
        "p_prompt": nrm(ks[6], (DEPTH, BATCH, SEQ, PLE_DIM), 1.0),
        "p_sample": nrm(ks[7], (DEPTH, DEC_BATCH, DEC_SEQ, PLE_DIM), 1.0),
        "g_norm": 1.0 + nrm(ks[8], (DEPTH, D_MODEL), 0.02),
        "w_in": nrm(ks[9], (DEPTH, D_MODEL, N_IN), D_MODEL ** -0.5),
        "b_f": 2.0 + nrm(ks[10], (DEPTH, H_FOX), 0.5),
        "g_out_sb": 1.0 + nrm(ks[11], (DEPTH, D_SB), 0.02),
        "g_out_fox": 1.0 + nrm(ks[12], (DEPTH, D_FOX), 0.02),
        "w_out": nrm(ks[13], (DEPTH, D_MIX, D_MODEL), D_MIX ** -0.5),
        "w_ple": nrm(ks[14], (DEPTH, PLE_DIM, D_MODEL), PLE_DIM ** -0.5),
        "g_ple": 1.0 + nrm(ks[15], (DEPTH, D_MODEL), 0.02),
        "w_ple_gate": nrm(ks[16], (DEPTH, D_MODEL, D_MODEL), D_MODEL ** -0.5),
        "g_final": 1.0 + nrm(ks[17], (D_MODEL,), 0.02),
    }


def reference(x_prompt, x_sample, cache_sb_kv, cache_fox_kv, cache_fox_logf, page_table,
              p_prompt, p_sample, g_norm, w_in, b_f, g_out_sb, g_out_fox, w_out,
              w_ple, g_ple, w_ple_gate, g_final):
    hp, hs = x_prompt, x_sample
    sb_p, fox_p, lf_p, sb_s, fox_s, lf_s = [], [], [], [], [], []
    for i in range(DEPTH):
        qs, ks, vs, qf, kf, vf, zs, zf, lf = project(hp, g_norm[i], w_in[i], b_f[i])
        o_s, o_f = prompt_mixers(qs, ks, vs, qf, kf, vf, lf)
        hp = merge(hp, o_s, o_f, zs, zf, g_out_sb[i], g_out_fox[i], w_out[i],
                   p_prompt[i], w_ple[i], g_ple[i], w_ple_gate[i])
        sb_p.append(jnp.stack([ks, vs], axis=2))
        fox_p.append(jnp.stack([kf, vf], axis=2))
        lf_p.append(lf)
        qs, ks, vs, qf, kf, vf, zs, zf, lf = project(hs, g_norm[i], w_in[i], b_f[i])
        o_s, o_f = sample_mixers(qs, ks, vs, qf, kf, vf, lf, cache_sb_kv[i], cache_fox_kv[i],
                                 cache_fox_logf[i], page_table)
        hs = merge(hs, o_s, o_f, zs, zf, g_out_sb[i], g_out_fox[i], w_out[i],
                   p_sample[i], w_ple[i], g_ple[i], w_ple_gate[i])
        sb_s.append(jnp.stack([ks, vs], axis=2))
        fox_s.append(jnp.stack([kf, vf], axis=2))
        lf_s.append(lf)
    y_prompt = rmsnorm(hp, g_final)
    y_sample = rmsnorm(hs, g_final)
    return (y_prompt, y_sample, jnp.stack(sb_p), jnp.stack(fox_p), jnp.stack(lf_p),
            jnp.stack(sb_s), jnp.stack(fox_s), jnp.stack(lf_s))
```

```python
import functools

import jax
import jax.numpy as jnp
from jax import lax
from jax.experimental import pallas as pl
from jax.experimental.pallas import tpu as pltpu

F32 = jnp.float32
BF16 = jnp.bfloat16

HEAD_DIM = 64
EPS = 1e-6
NEG_BIG = -1e30

LANES = 128
MXU_DIM = 256
VMEM_LIMIT_BYTES = 56 * 1024 * 1024

ROW_BLOCK = 256
ATT_BLOCK = MXU_DIM
PAGES_PER_STEP = 4


def _softplus(z):
    return jnp.maximum(z, 0.0) + jnp.log(1.0 + jnp.exp(-jnp.abs(z)))


def _split2(x):
    hi = x.astype(BF16)
    lo = (x - hi.astype(F32)).astype(BF16)
    return hi, lo


def _split3(x):
    hi = x.astype(BF16)
    r = x - hi.astype(F32)
    mid = r.astype(BF16)
    lo = (r - mid.astype(F32)).astype(BF16)
    return hi, mid, lo


def _dot(a, b):
    return jnp.dot(a, b, preferred_element_type=F32)


def _dot_nt(a, b):
    return lax.dot_general(a, b, (((1,), (1,)), ((), ())), preferred_element_type=F32)


def _project_kernel(x_ref, g_ref, w_ref, wf_ref, bf_ref,
                    qs_ref, kvs_ref, zs_ref, qf_ref, kvf_ref, zf_ref, lf_ref, *, d_sb, d_fox, h_fox):
    x = x_ref[...]
    ms = jnp.mean(x * x, axis=-1, keepdims=True)
    u = (x * lax.rsqrt(ms + EPS) * g_ref[...]).astype(BF16)
    o = 0
    for ref, width in ((qs_ref, d_sb), (kvs_ref, 2 * d_sb), (zs_ref, d_sb),
                       (qf_ref, d_fox), (kvf_ref, 2 * d_fox), (zf_ref, d_fox)):
        ref[...] = _dot(u, w_ref[:, o:o + width])
        o += width
    f = _dot(u, wf_ref[...])[:, :h_fox] + bf_ref[...]
    lf_ref[...] = jnp.minimum(f, 0.0) - jnp.log(1.0 + jnp.exp(-jnp.abs(f)))


def _project(x, g_norm, w_main, w_f, b_f, *, d_sb, d_fox, h_fox):
    n, d_model = x.shape
    tm = ROW_BLOCK
    assert n % tm == 0
    row = lambda width: pl.BlockSpec((tm, width), lambda i: (i, 0))
    full = lambda a: pl.BlockSpec(a.shape, lambda i: (0,) * a.ndim)
    widths = (d_sb, 2 * d_sb, d_sb, d_fox, 2 * d_fox, d_fox, h_fox)
    return pl.pallas_call(
        functools.partial(_project_kernel, d_sb=d_sb, d_fox=d_fox, h_fox=h_fox),
        grid=(n // tm,),
        in_specs=[row(d_model), full(g_norm), full(w_main), full(w_f), full(b_f)],
        out_specs=[row(w) for w in widths],
        out_shape=[jax.ShapeDtypeStruct((n, w), F32) for w in widths],
        compiler_params=pltpu.CompilerParams(dimension_semantics=("parallel",),
                                             vmem_limit_bytes=VMEM_LIMIT_BYTES),
        name="project",
    )(x, g_norm, w_main, w_f, b_f)


def _cumsum_kernel(x_ref, o_ref, *, chunk):
    rows, n = x_ref.shape
    r = lax.broadcasted_iota(jnp.int32, (chunk, chunk), 0)
    c = lax.broadcasted_iota(jnp.int32, (chunk, chunk), 1)
    upper = jnp.where(r <= c, 1.0, 0.0).astype(BF16)
    carry = jnp.zeros((rows, 1), F32)
    for i in range(n // chunk):
        hi, mid, lo = _split3(x_ref[:, i * chunk:(i + 1) * chunk])
        local = _dot(hi, upper) + _dot(mid, upper) + _dot(lo, upper)
        o_ref[:, i * chunk:(i + 1) * chunk] = local + carry
        carry = carry + local[:, chunk - 1:chunk]


def _cumsum_rows(x):
    return pl.pallas_call(
        functools.partial(_cumsum_kernel, chunk=MXU_DIM),
        out_shape=jax.ShapeDtypeStruct(x.shape, F32),
        name="cumsum_rows",
    )(x)


def _suffix_matrix(n, strict):
    j = lax.broadcasted_iota(jnp.int32, (n, n), 0)
    s = lax.broadcasted_iota(jnp.int32, (n, n), 1)
    keep = (j > s) if strict else (j >= s)
    return jnp.where(keep, -1.0, 0.0).astype(BF16)


def _sb_prompt_kernel(q_ref, k_ref, v_ref, o_ref, *, blk):
    qi = pl.program_id(2)
    q = q_ref[0, 0]
    neg_suffix = _suffix_matrix(blk, strict=False)
    t_idx = lax.broadcasted_iota(jnp.int32, (blk, blk), 0)
    s_idx = lax.broadcasted_iota(jnp.int32, (blk, blk), 1)
    causal = s_idx < t_idx

    def tile(kb, carry, acc, mask):
        start = pl.multiple_of(kb * blk, blk)
        k = k_ref[0, 0, pl.ds(start, blk), :]
        v = v_ref[0, 0, pl.ds(start, blk), :]
        z = _dot_nt(q, k)
        sp = _softplus(z)
        if mask is not None:
            sp = jnp.where(mask, sp, 0.0)
        hi, lo = _split2(sp)
        tail = _dot(hi, neg_suffix) + _dot(lo, neg_suffix)
        w = jnp.exp(z + tail + carry)
        if mask is not None:
            w = jnp.where(mask, w, 0.0)
        acc = acc + _dot(w.astype(BF16), v)
        return carry + tail[:, 0:1], acc

    carry0 = jnp.zeros((blk, 1), F32)
    acc0 = jnp.zeros((blk, q.shape[-1]), F32)
    carry, acc = tile(qi, carry0, acc0, causal)

    def body(i, state):
        return tile(qi - 1 - i, state[0], state[1], None)

    carry, acc = lax.fori_loop(0, qi, body, (carry, acc))
    o_ref[0, 0] = acc


def _fox_prompt_kernel(q_ref, k_ref, v_ref, cq_ref, ck_ref, o_ref, *, blk):
    qi = pl.program_id(2)
    q = q_ref[0, 0]
    cq = cq_ref[0, 0]
    t_idx = lax.broadcasted_iota(jnp.int32, (blk, blk), 0)
    s_idx = lax.broadcasted_iota(jnp.int32, (blk, blk), 1)
    causal = s_idx <= t_idx

    def tile(kb, m, l, acc, mask):
        start = pl.multiple_of(kb * blk, blk)
        k = k_ref[0, 0, pl.ds(start, blk), :]
        v = v_ref[0, 0, pl.ds(start, blk), :]
        ck = ck_ref[0, 0, :, pl.ds(start, blk)]
        logit = _dot_nt(q, k) + (cq - ck)
        if mask is not None:
            logit = jnp.where(mask, logit, NEG_BIG)
        m_new = jnp.maximum(m, jnp.max(logit, axis=-1, keepdims=True))
        alpha = jnp.exp(m - m_new)
        p = jnp.exp(logit - m_new)
        l = alpha * l + jnp.sum(p, axis=-1, keepdims=True)
        acc = alpha * acc + _dot(p.astype(BF16), v)
        return m_new, l, acc

    m0 = jnp.full((blk, 1), NEG_BIG, F32)
    l0 = jnp.zeros((blk, 1), F32)
    acc0 = jnp.zeros((blk, q.shape[-1]), F32)
    state = tile(qi, m0, l0, acc0, causal)

    def body(i, state):
        return tile(qi - 1 - i, *state, None)

    m, l, acc = lax.fori_loop(0, qi, body, state)
    o_ref[0, 0] = acc / l


def _prompt_attention(kernel, q, k, v, extra=()):
    b, h, s, d = q.shape
    blk = ATT_BLOCK
    assert s % blk == 0
    qspec = pl.BlockSpec((1, 1, blk, d), lambda bi, hi, qi: (bi, hi, qi, 0))
    kvspec = pl.BlockSpec((1, 1, s, d), lambda bi, hi, qi: (bi, hi, 0, 0))
    in_specs = [qspec, kvspec, kvspec]
    if extra:
        in_specs += [pl.BlockSpec((1, 1, blk, 1), lambda bi, hi, qi: (bi, hi, qi, 0)),
                     pl.BlockSpec((1, 1, 1, s), lambda bi, hi, qi: (bi, hi, 0, 0))]
    return pl.pallas_call(
        functools.partial(kernel, blk=blk),
        grid=(b, h, s // blk),
        in_specs=in_specs,
        out_specs=qspec,
        out_shape=jax.ShapeDtypeStruct((b, h, s, d), F32),
        compiler_params=pltpu.CompilerParams(
            dimension_semantics=("parallel", "parallel", "arbitrary"),
            vmem_limit_bytes=VMEM_LIMIT_BYTES),
        name=kernel.__name__.strip("_"),
    )(q, k, v, *extra)


def _sample_kernel(pt_ref, qs_ref, qf_ref, kvs_new_ref, kvf_new_ref, lf_new_ref, *refs,
                   n_pages_step, page, t_new, n_heads, d_mix):
    npg = n_pages_step
    sb_pages = refs[0:npg]
    fox_pages = refs[npg:2 * npg]
    lf_pages = refs[2 * npg:3 * npg]
    os_ref, of_ref = refs[3 * npg:3 * npg + 2]
    qs_s, qf_s, carry_s, acc_s, carry_f, m_f, l_f, acc_f = refs[3 * npg + 2:]
    step = pl.program_id(1)
    n_cols = n_heads * t_new
    d = d_mix // n_heads

    def block_diag_queries(q):
        reps = jnp.concatenate([q] * (LANES // t_new), axis=0)
        r = lax.broadcasted_iota(jnp.int32, (LANES, d_mix), 0)
        c = lax.broadcasted_iota(jnp.int32, (LANES, d_mix), 1)
        return jnp.where(r // t_new == c // d, reps, 0.0)

    j_idx = lax.broadcasted_iota(jnp.int32, (page, page), 1)
    s_idx = lax.broadcasted_iota(jnp.int32, (page, page), 0)
    neg_suffix_l = jnp.where(j_idx >= s_idx, -1.0, 0.0).astype(BF16)
    after_l = jnp.where(j_idx > s_idx, 1.0, 0.0).astype(BF16)
    hh = lax.broadcasted_iota(jnp.int32, (n_heads, LANES), 0)
    cc = lax.broadcasted_iota(jnp.int32, (n_heads, LANES), 1)
    expand = jnp.where(cc // t_new == hh, 1.0, 0.0).astype(BF16)
    key_row = lax.broadcasted_iota(jnp.int32, (page, LANES), 0)
    col_t = lax.broadcasted_iota(jnp.int32, (page, LANES), 1) % t_new

    def process(kv_s, kv_f, lf, is_new):
        k_s, v_s = kv_s[:, :d_mix], kv_s[:, d_mix:]
        z = _dot_nt(k_s, qs_s[...])
        sp = _softplus(z)
        if is_new:
            mask_s = key_row < col_t
            sp = jnp.where(mask_s, sp, 0.0)
        hi, lo = _split2(sp)
        tail = _dot(neg_suffix_l, hi) + _dot(neg_suffix_l, lo)
        w = jnp.exp(z + tail + carry_s[...])
        if is_new:
            w = jnp.where(mask_s, w, 0.0)
        acc_s[...] += _dot(w.T[:n_cols], v_s)
        carry_s[...] += tail[0:1, :]
        k_f, v_f = kv_f[:, :d_mix], kv_f[:, d_mix:]
        l_hi, l_mid, l_lo = _split3(lf)
        lf_cols = _dot(l_hi, expand) + _dot(l_mid, expand) + _dot(l_lo, expand)
        c_hi, c_mid, c_lo = _split3(lf_cols)
        after = _dot(after_l, c_hi) + _dot(after_l, c_mid) + _dot(after_l, c_lo)
        if is_new:
            carry_f[...] = -jnp.sum(jnp.where(key_row == col_t, after, 0.0), axis=0, keepdims=True)
        logit = _dot_nt(k_f, qf_s[...]) + after + carry_f[...]
        if is_new:
            logit = jnp.where(key_row <= col_t, logit, NEG_BIG)
        lt = logit.T[:n_cols]
        m_new = jnp.maximum(m_f[...], jnp.max(lt, axis=-1, keepdims=True))
        alpha = jnp.exp(m_f[...] - m_new)
        p = jnp.exp(lt - m_new)
        l_f[...] = alpha * l_f[...] + jnp.sum(p, axis=-1, keepdims=True)
        acc_f[...] = alpha * acc_f[...] + _dot(p, v_f)
        m_f[...] = m_new
        carry_f[...] += after[0:1, :] + lf_cols[0:1, :]

    @pl.when(step == 0)
    def _():
        qs_s[...] = block_diag_queries(qs_ref[...])
        qf_s[...] = block_diag_queries(qf_ref[...])
        carry_s[...] = jnp.zeros_like(carry_s)
        acc_s[...] = jnp.zeros_like(acc_s)
        m_f[...] = jnp.full_like(m_f, NEG_BIG)
        l_f[...] = jnp.zeros_like(l_f)
        acc_f[...] = jnp.zeros_like(acc_f)
        pad = lambda a: jnp.concatenate([a, jnp.zeros((page - t_new, a.shape[1]), F32)], axis=0)
        process(pad(kvs_new_ref[...]), pad(kvf_new_ref[...]), pad(lf_new_ref[...]), True)

    for i in range(npg):
        process(sb_pages[i][0], fox_pages[i][0], lf_pages[i][0], False)

    @pl.when(step == pl.num_programs(1) - 1)
    def _():
        def head_diag(acc):
            out = jnp.zeros((t_new, d_mix), F32)
            c = lax.broadcasted_iota(jnp.int32, (t_new, d_mix), 1)
            for h in range(n_heads):
                out = jnp.where(c // d == h, acc[h * t_new:(h + 1) * t_new], out)
            return out
        os_ref[...] = head_diag(acc_s[...])
        of_ref[...] = head_diag(acc_f[...] / l_f[...])


def _sample_attend(page_table, q_sb, q_fox, kv_sb_new, kv_fox_new, lf_new, pool_sb, pool_fox, pool_lf,
                   *, t_new, n_heads):
    n_seq, n_pages = page_table.shape
    page = pool_sb.shape[1]
    d_mix = q_sb.shape[1]
    npg = PAGES_PER_STEP
    assert n_pages % npg == 0 and n_heads * t_new <= LANES and LANES % t_new == 0
    n_cols = n_heads * t_new
    seq = lambda width: pl.BlockSpec((t_new, width), lambda b, j, pt: (b, 0))

    def page_spec(width, i):
        return pl.BlockSpec((1, page, width), lambda b, j, pt: (pt[b, n_pages - 1 - (j * npg + i)], 0, 0))

    in_specs = [seq(d_mix), seq(d_mix), seq(2 * d_mix), seq(2 * d_mix), seq(n_heads)]
    in_specs += [page_spec(2 * d_mix, i) for i in range(npg)]
    in_specs += [page_spec(2 * d_mix, i) for i in range(npg)]
    in_specs += [page_spec(n_heads, i) for i in range(npg)]
    scratch = [pltpu.VMEM((LANES, d_mix), F32), pltpu.VMEM((LANES, d_mix), F32),
               pltpu.VMEM((1, LANES), F32), pltpu.VMEM((n_cols, d_mix), F32),
               pltpu.VMEM((1, LANES), F32), pltpu.VMEM((n_cols, 1), F32),
               pltpu.VMEM((n_cols, 1), F32), pltpu.VMEM((n_cols, d_mix), F32)]
    grid_spec = pltpu.PrefetchScalarGridSpec(
        num_scalar_prefetch=1,
        grid=(n_seq, n_pages // npg),
        in_specs=in_specs,
        out_specs=[seq(d_mix), seq(d_mix)],
        scratch_shapes=scratch)
    return pl.pallas_call(
        functools.partial(_sample_kernel, n_pages_step=npg, page=page, t_new=t_new, n_heads=n_heads,
                          d_mix=d_mix),
        grid_spec=grid_spec,
        out_shape=[jax.ShapeDtypeStruct((n_seq * t_new, d_mix), F32)] * 2,
        compiler_params=pltpu.CompilerParams(dimension_semantics=("parallel", "arbitrary"),
                                             vmem_limit_bytes=VMEM_LIMIT_BYTES),
        name="sample_attend",
    )(page_table, q_sb, q_fox, kv_sb_new, kv_fox_new, lf_new,
      *([pool_sb] * npg), *([pool_fox] * npg), *([pool_lf] * npg))


def _merge_kernel(h_ref, os_ref, of_ref, zs_ref, zf_ref, p_ref, gs_ref, gf_ref, wo_ref, wp_ref, gp_ref,
                  wg_ref, gfin_ref, y_ref, *, head_dim):
    def head_norm_gate(o, g, z):
        width = o.shape[-1]
        r = lax.broadcasted_iota(jnp.int32, (width, width), 0) // head_dim
        c = lax.broadcasted_iota(jnp.int32, (width, width), 1) // head_dim
        same_head = jnp.where(r == c, 1.0, 0.0).astype(BF16)
        hi, lo = _split2(o * o)
        ms = (_dot(hi, same_head) + _dot(lo, same_head)) * (1.0 / head_dim)
        y = o * lax.rsqrt(ms + EPS) * g
        return y * (z * jax.nn.sigmoid(z))

    def rms(x, g):
        return x * lax.rsqrt(jnp.mean(x * x, axis=-1, keepdims=True) + EPS) * g

    a = head_norm_gate(os_ref[...], gs_ref[...], zs_ref[...])
    b = head_norm_gate(of_ref[...], gf_ref[...], zf_ref[...])
    d_sb = a.shape[-1]
    h = h_ref[...] + _dot(a.astype(BF16), wo_ref[:d_sb, :]) + _dot(b.astype(BF16), wo_ref[d_sb:, :])
    gate = jax.nn.sigmoid(_dot(rms(h, gp_ref[...]).astype(BF16), wg_ref[...]))
    h = h + _dot(p_ref[...].astype(BF16), wp_ref[...]) * gate
    y_ref[...] = rms(h, gfin_ref[...])


def _merge(h, o_sb, o_fox, z_sb, z_fox, p, g_out_sb, g_out_fox, w_out, w_ple, g_ple, w_gate, g_final):
    n, d_model = h.shape
    tm = ROW_BLOCK
    assert n % tm == 0
    row = lambda a: pl.BlockSpec((tm, a.shape[1]), lambda i: (i, 0))
    full = lambda a: pl.BlockSpec(a.shape, lambda i: (0,) * a.ndim)
    rows = (h, o_sb, o_fox, z_sb, z_fox, p)
    params = (g_out_sb, g_out_fox, w_out, w_ple, g_ple, w_gate, g_final)
    return pl.pallas_call(
        functools.partial(_merge_kernel, head_dim=HEAD_DIM),
        grid=(n // tm,),
        in_specs=[row(a) for a in rows] + [full(a) for a in params],
        out_specs=pl.BlockSpec((tm, d_model), lambda i: (i, 0)),
        out_shape=jax.ShapeDtypeStruct((n, d_model), F32),
        compiler_params=pltpu.CompilerParams(dimension_semantics=("parallel",),
                                             vmem_limit_bytes=VMEM_LIMIT_BYTES),
        name="merge",
    )(*rows, *params)


def kernel(x_prompt, x_sample, cache_sb_kv, cache_fox_kv, cache_fox_logf, page_table, p_prompt, p_sample,
           g_norm, w_in, b_f, g_out_sb, g_out_fox, w_out, w_ple, g_ple, w_ple_gate, g_final):
    depth = g_norm.shape[0]
    assert depth == 1, "single-layer decoder step"
    b, s, d_model = x_prompt.shape
    bd, t_new, _ = x_sample.shape
    h_sb, h_fox = cache_sb_kv.shape[4], cache_fox_kv.shape[4]
    d_sb, d_fox = h_sb * HEAD_DIM, h_fox * HEAD_DIM
    assert d_sb == d_fox and h_sb == h_fox
    n_main = 4 * d_sb + 4 * d_fox
    scale = HEAD_DIM ** -0.5
    row2 = lambda a: a.reshape(1, -1)

    w_main = w_in[0, :, :n_main].astype(BF16)
    w_f = jnp.pad(w_in[0, :, n_main:], ((0, 0), (0, LANES - h_fox))).astype(BF16)
    proj = functools.partial(_project, g_norm=row2(g_norm[0]), w_main=w_main, w_f=w_f, b_f=row2(b_f[0]),
                             d_sb=d_sb, d_fox=d_fox, h_fox=h_fox)
    merge = functools.partial(
        _merge, g_out_sb=row2(g_out_sb[0]), g_out_fox=row2(g_out_fox[0]), w_out=w_out[0].astype(BF16),
        w_ple=w_ple[0].astype(BF16), g_ple=row2(g_ple[0]), w_gate=w_ple_gate[0].astype(BF16),
        g_final=row2(g_final))

    xp = x_prompt.reshape(b * s, d_model)
    qs, kvs, zs, qf, kvf, zf, lf = proj(xp)

    def heads(t, n_heads, scl=None):
        t = t.reshape(b, s, n_heads, HEAD_DIM)
        if scl is not None:
            t = t * scl
        return jnp.transpose(t, (0, 2, 1, 3)).astype(BF16)

    o_s = _prompt_attention(_sb_prompt_kernel, heads(qs, h_sb, scale), heads(kvs[:, :d_sb], h_sb),
                            heads(kvs[:, d_sb:], h_sb))
    c_t = _cumsum_rows(jnp.transpose(lf.reshape(b, s, h_fox), (0, 2, 1)).reshape(b * h_fox, s))
    c_t = c_t.reshape(b, h_fox, s)
    o_f = _prompt_attention(_fox_prompt_kernel, heads(qf, h_fox, scale), heads(kvf[:, :d_fox], h_fox),
                            heads(kvf[:, d_fox:], h_fox), extra=(c_t[..., None], c_t[:, :, None, :]))
    unheads = lambda t: jnp.transpose(t, (0, 2, 1, 3)).reshape(b * s, -1)
    y_prompt = merge(xp, unheads(o_s), unheads(o_f), zs, zf, p_prompt[0].reshape(b * s, -1))

    xs = x_sample.reshape(bd * t_new, d_model)
    qs2, kvs2, zs2, qf2, kvf2, zf2, lf2 = proj(xs)
    n_pool, page = cache_sb_kv.shape[1], cache_sb_kv.shape[2]
    o_s2, o_f2 = _sample_attend(
        page_table, qs2 * scale, qf2 * scale, kvs2, kvf2, lf2,
        cache_sb_kv[0].reshape(n_pool, page, 2 * d_sb), cache_fox_kv[0].reshape(n_pool, page, 2 * d_fox),
        cache_fox_logf[0], t_new=t_new, n_heads=h_sb)
    y_sample = merge(xs, o_s2, o_f2, zs2, zf2, p_sample[0].reshape(bd * t_new, -1))

    return (y_prompt.reshape(b, s, d_model), y_sample.reshape(bd, t_new, d_model),
            kvs.reshape(1, b, s, 2, h_sb, HEAD_DIM), kvf.reshape(1, b, s, 2, h_fox, HEAD_DIM),
            lf.reshape(1, b, s, h_fox),
            kvs2.reshape(1, bd, t_new, 2, h_sb, HEAD_DIM), kvf2.reshape(1, bd, t_new, 2, h_fox, HEAD_DIM),
            lf2.reshape(1, bd, t_new, h_fox))
```

```python
import functools

import jax
import jax.numpy as jnp
from jax import lax
from jax.experimental import pallas as pl
from jax.experimental.pallas import tpu as pltpu

F32 = jnp.float32
BF16 = jnp.bfloat16

HEAD_DIM = 64
EPS = 1e-6
NEG_BIG = -1e30

LANES = 128
MXU_DIM = 256
VMEM_LIMIT_BYTES = 56 * 1024 * 1024

ROW_BLOCK = 256
ATT_BLOCK = MXU_DIM
ATT_LANE_GROUPS = 2
PAGES_PER_STEP = 8


def _softplus(z):
    return jnp.maximum(z, 0.0) + jnp.log(1.0 + jnp.exp(-jnp.abs(z)))


def _split2(x):
    hi = x.astype(BF16)
    lo = (x - hi.astype(F32)).astype(BF16)
    return hi, lo


def _split3(x):
    hi = x.astype(BF16)
    r = x - hi.astype(F32)
    mid = r.astype(BF16)
    lo = (r - mid.astype(F32)).astype(BF16)
    return hi, mid, lo


def _dot(a, b):
    return jnp.dot(a, b, preferred_element_type=F32)


def _dot_nt(a, b):
    return lax.dot_general(a, b, (((1,), (1,)), ((), ())), preferred_element_type=F32)


def _project_kernel(x_ref, g_ref, w_ref, wf_ref, bf_ref,
                    kvs_ref, zs_ref, kvf_ref, zf_ref, lf_ref, qs16_ref, kvs16_ref, qf16_ref, kvf16_ref,
                    *, d_sb, d_fox, h_fox, scale):
    x = x_ref[...]
    ms = jnp.mean(x * x, axis=-1, keepdims=True)
    u = (x * lax.rsqrt(ms + EPS) * g_ref[...]).astype(BF16)
    o = 0
    for q16_ref, kv_ref, kv16_ref, z_ref, width in ((qs16_ref, kvs_ref, kvs16_ref, zs_ref, d_sb),
                                                    (qf16_ref, kvf_ref, kvf16_ref, zf_ref, d_fox)):
        q16_ref[...] = (_dot(u, w_ref[:, o:o + width]) * scale).astype(BF16)
        kv = _dot(u, w_ref[:, o + width:o + 3 * width])
        kv_ref[...] = kv
        kv16_ref[...] = kv.astype(BF16)
        z_ref[...] = _dot(u, w_ref[:, o + 3 * width:o + 4 * width])
        o += 4 * width
    f = _dot(u, wf_ref[...])[:, :h_fox] + bf_ref[...]
    lf_ref[...] = jnp.minimum(f, 0.0) - jnp.log(1.0 + jnp.exp(-jnp.abs(f)))


def _project(x, g_norm, w_main, w_f, b_f, *, d_sb, d_fox, h_fox, scale):
    n, d_model = x.shape
    tm = ROW_BLOCK
    assert n % tm == 0
    row = lambda width: pl.BlockSpec((tm, width), lambda i: (i, 0))
    full = lambda a: pl.BlockSpec(a.shape, lambda i: (0,) * a.ndim)
    outs = ((2 * d_sb, F32), (d_sb, F32), (2 * d_fox, F32), (d_fox, F32), (h_fox, F32),
            (d_sb, BF16), (2 * d_sb, BF16), (d_fox, BF16), (2 * d_fox, BF16))
    return pl.pallas_call(
        functools.partial(_project_kernel, d_sb=d_sb, d_fox=d_fox, h_fox=h_fox, scale=scale),
        grid=(n // tm,),
        in_specs=[row(d_model), full(g_norm), full(w_main), full(w_f), full(b_f)],
        out_specs=[row(w) for w, _ in outs],
        out_shape=[jax.ShapeDtypeStruct((n, w), dt) for w, dt in outs],
        compiler_params=pltpu.CompilerParams(dimension_semantics=("parallel",),
                                             vmem_limit_bytes=VMEM_LIMIT_BYTES),
        name="project",
    )(x, g_norm, w_main, w_f, b_f)


def _cumsum_kernel(x_ref, o_ref, *, chunk):
    rows, n = x_ref.shape
    r = lax.broadcasted_iota(jnp.int32, (chunk, chunk), 0)
    c = lax.broadcasted_iota(jnp.int32, (chunk, chunk), 1)
    upper = jnp.where(r <= c, 1.0, 0.0).astype(BF16)
    carry = jnp.zeros((rows, 1), F32)
    for i in range(n // chunk):
        hi, mid, lo = _split3(x_ref[:, i * chunk:(i + 1) * chunk])
        local = _dot(hi, upper) + _dot(mid, upper) + _dot(lo, upper)
        o_ref[:, i * chunk:(i + 1) * chunk] = local + carry
        carry = carry + local[:, chunk - 1:chunk]


def _cumsum_rows(x):
    return pl.pallas_call(
        functools.partial(_cumsum_kernel, chunk=MXU_DIM),
        out_shape=jax.ShapeDtypeStruct(x.shape, F32),
        name="cumsum_rows",
    )(x)


def _suffix_matrix(n, strict):
    j = lax.broadcasted_iota(jnp.int32, (n, n), 0)
    s = lax.broadcasted_iota(jnp.int32, (n, n), 1)
    keep = (j > s) if strict else (j >= s)
    return jnp.where(keep, -1.0, 0.0).astype(BF16)


def _per_head_queries(q_ref, d):
    q = q_ref[...].astype(F32)
    out = []
    for g in range(q.shape[1] // LANES):
        qg = q[:, g * LANES:(g + 1) * LANES]
        lane = lax.broadcasted_iota(jnp.int32, qg.shape, 1)
        out += [jnp.where(lane // d == i, qg, 0.0).astype(BF16) for i in range(LANES // d)]
    return out


def _lane_group(i, d):
    g = i // (LANES // d)
    return slice(g * LANES, (g + 1) * LANES)


def _select_head_lanes(accs, d):
    per = LANES // d
    lane = lax.broadcasted_iota(jnp.int32, accs[0].shape, 1)
    groups = []
    for g in range(len(accs) // per):
        out = accs[g * per]
        for i in range(1, per):
            out = jnp.where(lane // d == i, accs[g * per + i], out)
        groups.append(out)
    return jnp.concatenate(groups, axis=1)


def _sb_prompt_kernel(q_ref, k_ref, v_ref, o_ref, *, blk, d):
    qi = pl.program_id(2)
    qs = _per_head_queries(q_ref, d)
    nh = len(qs)
    neg_suffix = _suffix_matrix(blk, strict=False)
    t_idx = lax.broadcasted_iota(jnp.int32, (blk, blk), 0)
    s_idx = lax.broadcasted_iota(jnp.int32, (blk, blk), 1)
    causal = s_idx < t_idx

    def tile(kb, state, mask):
        start = pl.multiple_of(kb * blk, blk)
        k = k_ref[pl.ds(start, blk), :]
        v = v_ref[pl.ds(start, blk), :]
        zs = [_dot_nt(qs[i], k[:, _lane_group(i, d)]) for i in range(nh)]
        sps = [_softplus(z) for z in zs]
        if mask is not None:
            sps = [jnp.where(mask, sp, 0.0) for sp in sps]
        parts = [_split2(sp) for sp in sps]
        tails = [_dot(hi, neg_suffix) + _dot(lo, neg_suffix) for hi, lo in parts]
        ws = [jnp.exp(zs[i] + tails[i] + state[2 * i]) for i in range(nh)]
        if mask is not None:
            ws = [jnp.where(mask, w, 0.0) for w in ws]
        pvs = [_dot(ws[i].astype(BF16), v[:, _lane_group(i, d)]) for i in range(nh)]
        new = []
        for i in range(nh):
            new += [state[2 * i] + tails[i][:, 0:1], state[2 * i + 1] + pvs[i]]
        return tuple(new)

    init = (jnp.zeros((blk, 1), F32), jnp.zeros((blk, LANES), F32)) * nh
    state = tile(qi, init, causal)
    state = lax.fori_loop(0, qi, lambda i, st: tile(qi - 1 - i, st, None), state)
    o_ref[...] = _select_head_lanes([state[2 * i + 1] for i in range(nh)], d)


def _fox_prompt_kernel(q_ref, k_ref, v_ref, cq_ref, ck_ref, o_ref, *, blk, d):
    qi = pl.program_id(2)
    qs = _per_head_queries(q_ref, d)
    nh = len(qs)
    cq = cq_ref[0, 0]
    t_idx = lax.broadcasted_iota(jnp.int32, (blk, blk), 0)
    s_idx = lax.broadcasted_iota(jnp.int32, (blk, blk), 1)
    causal = s_idx <= t_idx

    def tile(kb, state, mask):
        start = pl.multiple_of(kb * blk, blk)
        k = k_ref[pl.ds(start, blk), :]
        v = v_ref[pl.ds(start, blk), :]
        ck = ck_ref[0, 0, :, pl.ds(start, blk)]
        logits = [_dot_nt(qs[i], k[:, _lane_group(i, d)]) + (cq[:, i:i + 1] - ck[i:i + 1, :])
                  for i in range(nh)]
        if mask is not None:
            logits = [jnp.where(mask, logit, NEG_BIG) for logit in logits]
        m_new = [jnp.maximum(state[3 * i], jnp.max(logits[i], axis=-1, keepdims=True)) for i in range(nh)]
        ps = [jnp.exp(logits[i] - m_new[i]) for i in range(nh)]
        pvs = [_dot(ps[i].astype(BF16), v[:, _lane_group(i, d)]) for i in range(nh)]
        new = []
        for i in range(nh):
            alpha = jnp.exp(state[3 * i] - m_new[i])
            new += [m_new[i], alpha * state[3 * i + 1] + jnp.sum(ps[i], axis=-1, keepdims=True),
                    alpha * state[3 * i + 2] + pvs[i]]
        return tuple(new)

    init = (jnp.full((blk, 1), NEG_BIG, F32), jnp.zeros((blk, 1), F32), jnp.zeros((blk, LANES), F32)) * nh
    state = tile(qi, init, causal)
    state = lax.fori_loop(0, qi, lambda i, st: tile(qi - 1 - i, st, None), state)
    o_ref[...] = _select_head_lanes([state[3 * i + 2] / state[3 * i + 1] for i in range(nh)], d)


def _prompt_attention(kernel, q16, kv16, *, batch, extra=()):
    n, d_mix = q16.shape
    s = n // batch
    blk, width = ATT_BLOCK, ATT_LANE_GROUPS * LANES
    assert s % blk == 0 and d_mix % width == 0
    groups, nq = d_mix // width, s // blk
    qspec = pl.BlockSpec((blk, width), lambda bi, g, qi: (bi * nq + qi, g))
    kspec = pl.BlockSpec((s, width), lambda bi, g, qi: (bi, g))
    vspec = pl.BlockSpec((s, width), lambda bi, g, qi: (bi, groups + g))
    in_specs = [qspec, kspec, vspec]
    if extra:
        per = extra[0].shape[-1]
        in_specs += [pl.BlockSpec((1, 1, blk, per), lambda bi, g, qi: (bi, g, qi, 0)),
                     pl.BlockSpec((1, 1, per, s), lambda bi, g, qi: (bi, g, 0, 0))]
    return pl.pallas_call(
        functools.partial(kernel, blk=blk, d=HEAD_DIM),
        grid=(batch, groups, nq),
        in_specs=in_specs,
        out_specs=qspec,
        out_shape=jax.ShapeDtypeStruct((n, d_mix), F32),
        compiler_params=pltpu.CompilerParams(
            dimension_semantics=("parallel", "parallel", "arbitrary"),
            vmem_limit_bytes=VMEM_LIMIT_BYTES),
        name=kernel.__name__.strip("_"),
    )(q16, kv16, kv16, *extra)


def _split3_f32(x):
    hi = x.astype(BF16).astype(F32)
    r = x - hi
    mid = r.astype(BF16).astype(F32)
    return hi, mid, (r - mid).astype(BF16).astype(F32)


def _sample_kernel(pt_ref, qs_ref, qf_ref, kvs_new_ref, kvf_new_ref, lft_new_ref, *refs,
                   n_pages_step, page, t_new, n_heads, d_mix):
    npg = n_pages_step
    sb_pages = refs[0:npg]
    fox_pages = refs[npg:2 * npg]
    lf_pages = refs[2 * npg:3 * npg]
    os_ref, of_ref = refs[3 * npg:3 * npg + 2]
    qs_s, qf_s, carry_s, acc_s, carry_f, m_f, l_f, acc_f = refs[3 * npg + 2:]
    step = pl.program_id(1)
    n_rows = n_heads * t_new
    d = d_mix // n_heads

    def block_diag_queries(q):
        reps = jnp.concatenate([q] * n_heads, axis=0)
        r = lax.broadcasted_iota(jnp.int32, (n_rows, d_mix), 0)
        c = lax.broadcasted_iota(jnp.int32, (n_rows, d_mix), 1)
        return jnp.where(r // t_new == c // d, reps, 0.0)

    def rows_from_heads(a):
        return jnp.concatenate([jnp.broadcast_to(a[h:h + 1], (t_new, a.shape[1])) for h in range(n_heads)],
                               axis=0)

    j_idx = lax.broadcasted_iota(jnp.int32, (page, page), 0)
    s_idx = lax.broadcasted_iota(jnp.int32, (page, page), 1)
    neg_suffix = jnp.where(j_idx >= s_idx, -1.0, 0.0).astype(BF16)
    after_m = jnp.where(j_idx > s_idx, 1.0, 0.0)
    row_t = lax.broadcasted_iota(jnp.int32, (n_rows, page), 0) % t_new
    key = lax.broadcasted_iota(jnp.int32, (n_rows, page), 1)

    def attend(blocks, is_new):
        mask_s = key < row_t
        mask_f = key <= row_t
        zs = [blk[0](qs_s[...]) for blk in blocks]
        dots_f = [blk[2](qf_s[...]) for blk in blocks]
        sps = [_softplus(z) for z in zs]
        if is_new:
            sps = [jnp.where(mask_s, sp, 0.0) for sp in sps]
        parts = [_split2(sp) for sp in sps]
        tails = [_dot(hi, neg_suffix) + _dot(lo, neg_suffix) for hi, lo in parts]
        carry = carry_s[...]
        ws = []
        for z, tail in zip(zs, tails):
            w = jnp.exp(z + tail + carry)
            ws.append(jnp.where(mask_s, w, 0.0) if is_new else w)
            carry = carry + tail[:, 0:1]
        carry_s[...] = carry
        carry = carry_f[...]
        logits = []
        for dot_f, blk in zip(dots_f, blocks):
            lft = blk[4]
            hi, mid, lo = _split3_f32(lft)
            after_h = _dot(hi, after_m) + _dot(mid, after_m) + _dot(lo, after_m)
            after = rows_from_heads(after_h)
            total = rows_from_heads(after_h[:, 0:1] + lft[:, 0:1])
            if is_new:
                carry = -jnp.sum(jnp.where(key == row_t, after, 0.0), axis=-1, keepdims=True)
            logit = dot_f + after + carry
            logits.append(jnp.where(mask_f, logit, NEG_BIG) if is_new else logit)
            carry = carry + total
        carry_f[...] = carry
        top = logits[0]
        for logit in logits[1:]:
            top = jnp.maximum(top, logit)
        m_old = m_f[...]
        m_new = jnp.maximum(m_old, jnp.max(top, axis=-1, keepdims=True))
        alpha = jnp.exp(m_old - m_new)
        ps = [jnp.exp(logit - m_new) for logit in logits]
        pv_s = [blk[1](w) for blk, w in zip(blocks, ws)]
        pv_f = [blk[3](p) for blk, p in zip(blocks, ps)]
        acc_s[...] += functools.reduce(lambda a, b: a + b, pv_s)
        l_f[...] = alpha * l_f[...] + jnp.sum(functools.reduce(lambda a, b: a + b, ps), axis=-1, keepdims=True)
        acc_f[...] = alpha * acc_f[...] + functools.reduce(lambda a, b: a + b, pv_f)
        m_f[...] = m_new

    @pl.when(step == 0)
    def _():
        qs_s[...] = block_diag_queries(qs_ref[...])
        qf_s[...] = block_diag_queries(qf_ref[...])
        carry_s[...] = jnp.zeros_like(carry_s)
        acc_s[...] = jnp.zeros_like(acc_s)
        carry_f[...] = jnp.zeros_like(carry_f)
        m_f[...] = jnp.full_like(m_f, NEG_BIG)
        l_f[...] = jnp.zeros_like(l_f)
        acc_f[...] = jnp.zeros_like(acc_f)
        pad = lambda a: jnp.concatenate([a, jnp.zeros((page - t_new, a.shape[1]), F32)], axis=0)
        kv_s, kv_f = pad(kvs_new_ref[...]), pad(kvf_new_ref[...])
        attend([(lambda q: _dot_nt(q, kv_s[:, :d_mix]), lambda w: _dot(w, kv_s[:, d_mix:]),
                 lambda q: _dot_nt(q, kv_f[:, :d_mix]), lambda w: _dot(w, kv_f[:, d_mix:]),
                 lft_new_ref[0])], True)

    def page_block(i):
        ks, vs = sb_pages[i].at[0, 0:d_mix, :], sb_pages[i].at[0, d_mix:2 * d_mix, :]
        kf, vf = fox_pages[i].at[0, 0:d_mix, :], fox_pages[i].at[0, d_mix:2 * d_mix, :]
        return (lambda q: _dot(q, ks[...]), lambda w: _dot_nt(w, vs[...]),
                lambda q: _dot(q, kf[...]), lambda w: _dot_nt(w, vf[...]), lf_pages[i][0])

    attend([page_block(i) for i in range(npg)], False)

    @pl.when(step == pl.num_programs(1) - 1)
    def _():
        def head_diag(acc):
            out = jnp.zeros((t_new, d_mix), F32)
            c = lax.broadcasted_iota(jnp.int32, (t_new, d_mix), 1)
            for h in range(n_heads):
                out = jnp.where(c // d == h, acc[h * t_new:(h + 1) * t_new], out)
            return out
        os_ref[...] = head_diag(acc_s[...])
        of_ref[...] = head_diag(acc_f[...] / l_f[...])


def _sample_attend(page_table, q_sb, q_fox, kv_sb_new, kv_fox_new, lft_new, pool_sb, pool_fox, pool_lf,
                   *, t_new, n_heads):
    n_seq, n_pages = page_table.shape
    page = pool_sb.shape[2]
    d_mix = q_sb.shape[1]
    npg = PAGES_PER_STEP
    assert n_pages % npg == 0
    n_rows = n_heads * t_new
    seq = lambda width: pl.BlockSpec((t_new, width), lambda b, j, pt: (b, 0))

    def page_spec(rows, i):
        return pl.BlockSpec((1, rows, page), lambda b, j, pt: (pt[b, n_pages - 1 - (j * npg + i)], 0, 0))

    in_specs = [seq(d_mix), seq(d_mix), seq(2 * d_mix), seq(2 * d_mix),
                pl.BlockSpec((1, n_heads, page), lambda b, j, pt: (b, 0, 0))]
    in_specs += [page_spec(2 * d_mix, i) for i in range(npg)]
    in_specs += [page_spec(2 * d_mix, i) for i in range(npg)]
    in_specs += [page_spec(n_heads, i) for i in range(npg)]
    scratch = [pltpu.VMEM((n_rows, d_mix), F32), pltpu.VMEM((n_rows, d_mix), F32),
               pltpu.VMEM((n_rows, 1), F32), pltpu.VMEM((n_rows, d_mix), F32),
               pltpu.VMEM((n_rows, 1), F32), pltpu.VMEM((n_rows, 1), F32),
               pltpu.VMEM((n_rows, 1), F32), pltpu.VMEM((n_rows, d_mix), F32)]
    grid_spec = pltpu.PrefetchScalarGridSpec(
        num_scalar_prefetch=1,
        grid=(n_seq, n_pages // npg),
        in_specs=in_specs,
        out_specs=[seq(d_mix), seq(d_mix)],
        scratch_shapes=scratch)
    return pl.pallas_call(
        functools.partial(_sample_kernel, n_pages_step=npg, page=page, t_new=t_new, n_heads=n_heads,
                          d_mix=d_mix),
        grid_spec=grid_spec,
        out_shape=[jax.ShapeDtypeStruct((n_seq * t_new, d_mix), F32)] * 2,
        compiler_params=pltpu.CompilerParams(dimension_semantics=("parallel", "arbitrary"),
                                             vmem_limit_bytes=VMEM_LIMIT_BYTES),
        name="sample_attend",
    )(page_table, q_sb, q_fox, kv_sb_new, kv_fox_new, lft_new,
      *([pool_sb] * npg), *([pool_fox] * npg), *([pool_lf] * npg))


def _merge_kernel(h_ref, os_ref, of_ref, zs_ref, zf_ref, p_ref, gs_ref, gf_ref, wo_ref, wp_ref, gp_ref,
                  wg_ref, gfin_ref, y_ref, *, head_dim):
    def head_norm_gate(o, g, z):
        width = o.shape[-1]
        r = lax.broadcasted_iota(jnp.int32, (width, width), 0) // head_dim
        c = lax.broadcasted_iota(jnp.int32, (width, width), 1) // head_dim
        same_head = jnp.where(r == c, 1.0, 0.0).astype(BF16)
        hi, lo = _split2(o * o)
        ms = (_dot(hi, same_head) + _dot(lo, same_head)) * (1.0 / head_dim)
        y = o * lax.rsqrt(ms + EPS) * g
        return y * (z * jax.nn.sigmoid(z))

    def rms(x, g):
        return x * lax.rsqrt(jnp.mean(x * x, axis=-1, keepdims=True) + EPS) * g

    a = head_norm_gate(os_ref[...], gs_ref[...], zs_ref[...])
    b = head_norm_gate(of_ref[...], gf_ref[...], zf_ref[...])
    d_sb = a.shape[-1]
    h = h_ref[...] + _dot(a.astype(BF16), wo_ref[:d_sb, :]) + _dot(b.astype(BF16), wo_ref[d_sb:, :])
    gate = jax.nn.sigmoid(_dot(rms(h, gp_ref[...]).astype(BF16), wg_ref[...]))
    h = h + _dot(p_ref[...].astype(BF16), wp_ref[...]) * gate
    y_ref[...] = rms(h, gfin_ref[...])


def _merge(h, o_sb, o_fox, z_sb, z_fox, p, g_out_sb, g_out_fox, w_out, w_ple, g_ple, w_gate, g_final):
    n, d_model = h.shape
    tm = ROW_BLOCK
    assert n % tm == 0
    row = lambda a: pl.BlockSpec((tm, a.shape[1]), lambda i: (i, 0))
    full = lambda a: pl.BlockSpec(a.shape, lambda i: (0,) * a.ndim)
    rows = (h, o_sb, o_fox, z_sb, z_fox, p)
    params = (g_out_sb, g_out_fox, w_out, w_ple, g_ple, w_gate, g_final)
    return pl.pallas_call(
        functools.partial(_merge_kernel, head_dim=HEAD_DIM),
        grid=(n // tm,),
        in_specs=[row(a) for a in rows] + [full(a) for a in params],
        out_specs=pl.BlockSpec((tm, d_model), lambda i: (i, 0)),
        out_shape=jax.ShapeDtypeStruct((n, d_model), F32),
        compiler_params=pltpu.CompilerParams(dimension_semantics=("parallel",),
                                             vmem_limit_bytes=VMEM_LIMIT_BYTES),
        name="merge",
    )(*rows, *params)


def kernel(x_prompt, x_sample, cache_sb_kv, cache_fox_kv, cache_fox_logf, page_table, p_prompt, p_sample,
           g_norm, w_in, b_f, g_out_sb, g_out_fox, w_out, w_ple, g_ple, w_ple_gate, g_final):
    depth = g_norm.shape[0]
    assert depth == 1, "single-layer decoder step"
    b, s, d_model = x_prompt.shape
    bd, t_new, _ = x_sample.shape
    h_sb, h_fox = cache_sb_kv.shape[4], cache_fox_kv.shape[4]
    d_sb, d_fox = h_sb * HEAD_DIM, h_fox * HEAD_DIM
    assert d_sb == d_fox and h_sb == h_fox
    n_main = 4 * d_sb + 4 * d_fox
    scale = HEAD_DIM ** -0.5
    row2 = lambda a: a.reshape(1, -1)

    w_main = w_in[0, :, :n_main].astype(BF16)
    w_f = jnp.pad(w_in[0, :, n_main:], ((0, 0), (0, LANES - h_fox))).astype(BF16)
    proj = functools.partial(_project, g_norm=row2(g_norm[0]), w_main=w_main, w_f=w_f, b_f=row2(b_f[0]),
                             d_sb=d_sb, d_fox=d_fox, h_fox=h_fox, scale=scale)
    merge = functools.partial(
        _merge, g_out_sb=row2(g_out_sb[0]), g_out_fox=row2(g_out_fox[0]), w_out=w_out[0].astype(BF16),
        w_ple=w_ple[0].astype(BF16), g_ple=row2(g_ple[0]), w_gate=w_ple_gate[0].astype(BF16),
        g_final=row2(g_final))

    xp = x_prompt.reshape(b * s, d_model)
    kvs, zs, kvf, zf, lf, qs16, kvs16, qf16, kvf16 = proj(xp)
    o_s = _prompt_attention(_sb_prompt_kernel, qs16, kvs16, batch=b)
    c_t = _cumsum_rows(jnp.transpose(lf.reshape(b, s, h_fox), (0, 2, 1)).reshape(b * h_fox, s))
    per = ATT_LANE_GROUPS * LANES // HEAD_DIM
    ck = c_t.reshape(b, h_fox // per, per, s)
    o_f = _prompt_attention(_fox_prompt_kernel, qf16, kvf16, batch=b,
                            extra=(jnp.transpose(ck, (0, 1, 3, 2)), ck))
    y_prompt = merge(xp, o_s, o_f, zs, zf, p_prompt[0].reshape(b * s, -1))

    xs = x_sample.reshape(bd * t_new, d_model)
    kvs2, zs2, kvf2, zf2, lf2, qs2, _, qf2, _ = proj(xs)
    n_pool, page = cache_sb_kv.shape[1], cache_sb_kv.shape[2]
    pool_view = lambda c: jnp.transpose(c[0], (0, 2, 3, 4, 1)).reshape(n_pool, -1, page)
    lft_new = jnp.pad(jnp.transpose(lf2.reshape(bd, t_new, h_fox), (0, 2, 1)),
                      ((0, 0), (0, 0), (0, page - t_new)))
    o_s2, o_f2 = _sample_attend(
        page_table, qs2.astype(F32), qf2.astype(F32), kvs2, kvf2, lft_new,
        pool_view(cache_sb_kv), pool_view(cache_fox_kv), jnp.transpose(cache_fox_logf[0], (0, 2, 1)),
        t_new=t_new, n_heads=h_sb)
    y_sample = merge(xs, o_s2, o_f2, zs2, zf2, p_sample[0].reshape(bd * t_new, -1))

    return (y_prompt.reshape(b, s, d_model), y_sample.reshape(bd, t_new, d_model),
            kvs.reshape(1, b, s, 2, h_sb, HEAD_DIM), kvf.reshape(1, b, s, 2, h_fox, HEAD_DIM),
            lf.reshape(1, b, s, h_fox),
            kvs2.reshape(1, bd, t_new, 2, h_sb, HEAD_DIM), kvf2.reshape(1, bd, t_new, 2, h_fox, HEAD_DIM),
            lf2.reshape(1, bd, t_new, h_fox))
```

```python
import functools

import jax
import jax.numpy as jnp
from jax import lax
from jax.experimental import pallas as pl
from jax.experimental.pallas import tpu as pltpu

F32 = jnp.float32
BF16 = jnp.bfloat16

HEAD_DIM = 64
EPS = 1e-6
NEG_BIG = -1e30
F32_EXP_UNDERFLOW = -104.0
BOUND_SLACK = 1.0 + 2.0 ** -10

LANES = 128
MXU_DIM = 256
VMEM_LIMIT_BYTES = 56 * 1024 * 1024

ROW_BLOCK = 256
ATT_BLOCK = MXU_DIM
ATT_LANE_GROUPS = 2
PAGES_PER_STEP = 16


def _softplus(z):
    return jnp.maximum(z, 0.0) + jnp.log(1.0 + jnp.exp(-jnp.abs(z)))


def _split2(x):
    hi = x.astype(BF16)
    lo = (x - hi.astype(F32)).astype(BF16)
    return hi, lo


def _split3(x):
    hi = x.astype(BF16)
    r = x - hi.astype(F32)
    mid = r.astype(BF16)
    lo = (r - mid.astype(F32)).astype(BF16)
    return hi, mid, lo


def _dot(a, b):
    return jnp.dot(a, b, preferred_element_type=F32)


def _dot_nt(a, b):
    return lax.dot_general(a, b, (((1,), (1,)), ((), ())), preferred_element_type=F32)


def _project_kernel(x_ref, g_ref, w_ref, wf_ref, bf_ref,
                    kvs_ref, zs_ref, kvf_ref, zf_ref, lf_ref, qs16_ref, kvs16_ref, qf16_ref, kvf16_ref,
                    *, d_sb, d_fox, h_fox, scale):
    x = x_ref[...]
    ms = jnp.mean(x * x, axis=-1, keepdims=True)
    u = (x * lax.rsqrt(ms + EPS) * g_ref[...]).astype(BF16)
    o = 0
    for q16_ref, kv_ref, kv16_ref, z_ref, width in ((qs16_ref, kvs_ref, kvs16_ref, zs_ref, d_sb),
                                                    (qf16_ref, kvf_ref, kvf16_ref, zf_ref, d_fox)):
        q16_ref[...] = (_dot(u, w_ref[:, o:o + width]) * scale).astype(BF16)
        kv = _dot(u, w_ref[:, o + width:o + 3 * width])
        kv_ref[...] = kv
        kv16_ref[...] = kv.astype(BF16)
        z_ref[...] = _dot(u, w_ref[:, o + 3 * width:o + 4 * width])
        o += 4 * width
    f = _dot(u, wf_ref[...])[:, :h_fox] + bf_ref[...]
    lf_ref[...] = jnp.minimum(f, 0.0) - jnp.log(1.0 + jnp.exp(-jnp.abs(f)))


def _project(x, g_norm, w_main, w_f, b_f, *, d_sb, d_fox, h_fox, scale):
    n, d_model = x.shape
    tm = ROW_BLOCK
    assert n % tm == 0
    row = lambda width: pl.BlockSpec((tm, width), lambda i: (i, 0))
    full = lambda a: pl.BlockSpec(a.shape, lambda i: (0,) * a.ndim)
    outs = ((2 * d_sb, F32), (d_sb, F32), (2 * d_fox, F32), (d_fox, F32), (h_fox, F32),
            (d_sb, BF16), (2 * d_sb, BF16), (d_fox, BF16), (2 * d_fox, BF16))
    return pl.pallas_call(
        functools.partial(_project_kernel, d_sb=d_sb, d_fox=d_fox, h_fox=h_fox, scale=scale),
        grid=(n // tm,),
        in_specs=[row(d_model), full(g_norm), full(w_main), full(w_f), full(b_f)],
        out_specs=[row(w) for w, _ in outs],
        out_shape=[jax.ShapeDtypeStruct((n, w), dt) for w, dt in outs],
        compiler_params=pltpu.CompilerParams(dimension_semantics=("parallel",),
                                             vmem_limit_bytes=VMEM_LIMIT_BYTES),
        name="project",
    )(x, g_norm, w_main, w_f, b_f)


def _cumsum_kernel(x_ref, o_ref, *, chunk):
    rows, n = x_ref.shape
    r = lax.broadcasted_iota(jnp.int32, (chunk, chunk), 0)
    c = lax.broadcasted_iota(jnp.int32, (chunk, chunk), 1)
    upper = jnp.where(r <= c, 1.0, 0.0).astype(BF16)
    carry = jnp.zeros((rows, 1), F32)
    for i in range(n // chunk):
        hi, mid, lo = _split3(x_ref[:, i * chunk:(i + 1) * chunk])
        local = _dot(hi, upper) + _dot(mid, upper) + _dot(lo, upper)
        o_ref[:, i * chunk:(i + 1) * chunk] = local + carry
        carry = carry + local[:, chunk - 1:chunk]


def _cumsum_rows(x):
    return pl.pallas_call(
        functools.partial(_cumsum_kernel, chunk=MXU_DIM),
        out_shape=jax.ShapeDtypeStruct(x.shape, F32),
        name="cumsum_rows",
    )(x)


def _suffix_matrix(n, strict):
    j = lax.broadcasted_iota(jnp.int32, (n, n), 0)
    s = lax.broadcasted_iota(jnp.int32, (n, n), 1)
    keep = (j > s) if strict else (j >= s)
    return jnp.where(keep, -1.0, 0.0).astype(BF16)


def _per_head_queries(q_ref, d):
    q = q_ref[...].astype(F32)
    out = []
    for g in range(q.shape[1] // LANES):
        qg = q[:, g * LANES:(g + 1) * LANES]
        lane = lax.broadcasted_iota(jnp.int32, qg.shape, 1)
        out += [jnp.where(lane // d == i, qg, 0.0).astype(BF16) for i in range(LANES // d)]
    return out


def _lane_group(i, d):
    g = i // (LANES // d)
    return slice(g * LANES, (g + 1) * LANES)


def _select_head_lanes(accs, d):
    per = LANES // d
    lane = lax.broadcasted_iota(jnp.int32, accs[0].shape, 1)
    groups = []
    for g in range(len(accs) // per):
        out = accs[g * per]
        for i in range(1, per):
            out = jnp.where(lane // d == i, accs[g * per + i], out)
        groups.append(out)
    return jnp.concatenate(groups, axis=1)


def _sb_prompt_kernel(q_ref, k_ref, v_ref, o_ref, *, blk, d):
    qi = pl.program_id(2)
    qs = _per_head_queries(q_ref, d)
    nh = len(qs)
    neg_suffix = _suffix_matrix(blk, strict=False)
    t_idx = lax.broadcasted_iota(jnp.int32, (blk, blk), 0)
    s_idx = lax.broadcasted_iota(jnp.int32, (blk, blk), 1)
    causal = s_idx < t_idx

    def tile(kb, state, mask):
        start = pl.multiple_of(kb * blk, blk)
        k = k_ref[pl.ds(start, blk), :]
        v = v_ref[pl.ds(start, blk), :]
        zs = [_dot_nt(qs[i], k[:, _lane_group(i, d)]) for i in range(nh)]
        sps = [_softplus(z) for z in zs]
        if mask is not None:
            sps = [jnp.where(mask, sp, 0.0) for sp in sps]
        parts = [_split2(sp) for sp in sps]
        tails = [_dot(hi, neg_suffix) + _dot(lo, neg_suffix) for hi, lo in parts]
        ws = [jnp.exp(zs[i] + tails[i] + state[2 * i]) for i in range(nh)]
        if mask is not None:
            ws = [jnp.where(mask, w, 0.0) for w in ws]
        pvs = [_dot(ws[i].astype(BF16), v[:, _lane_group(i, d)]) for i in range(nh)]
        new = []
        for i in range(nh):
            new += [state[2 * i] + tails[i][:, 0:1], state[2 * i + 1] + pvs[i]]
        top = functools.reduce(jnp.maximum, new[0::2])
        more = (jnp.max(top) >= F32_EXP_UNDERFLOW).astype(jnp.int32)
        return more, tuple(new)

    init = (jnp.zeros((blk, 1), F32), jnp.zeros((blk, LANES), F32)) * nh
    more, state = tile(qi, init, causal)

    def body(loop):
        i, _, st = loop
        more, st = tile(qi - 1 - i, st, None)
        return i + 1, more, st

    _, _, state = lax.while_loop(lambda loop: jnp.logical_and(loop[0] < qi, loop[1] > 0), body,
                                 (jnp.int32(0), more, state))
    o_ref[...] = _select_head_lanes([state[2 * i + 1] for i in range(nh)], d)


def _fox_prompt_kernel(q_ref, k_ref, v_ref, cq_ref, ck_ref, o_ref, kmax_ref, *, blk, d):
    qi = pl.program_id(2)
    qs = _per_head_queries(q_ref, d)
    nh = len(qs)
    cq = cq_ref[0, 0]
    t_idx = lax.broadcasted_iota(jnp.int32, (blk, blk), 0)
    s_idx = lax.broadcasted_iota(jnp.int32, (blk, blk), 1)
    causal = s_idx <= t_idx
    head_lane = lax.broadcasted_iota(jnp.int32, (blk, LANES), 1) // d

    def head_sq_norm(x, i):
        return jnp.sum(jnp.where(head_lane == i % (LANES // d), x * x, 0.0), axis=-1, keepdims=True)

    @pl.when(qi == 0)
    def _():
        def chunk(c, best):
            kc = k_ref[pl.ds(pl.multiple_of(c * blk, blk), blk), :].astype(F32)
            return tuple(jnp.maximum(best[i], head_sq_norm(kc[:, _lane_group(i, d)], i)) for i in range(nh))

        best = lax.fori_loop(0, k_ref.shape[0] // blk, chunk, (jnp.zeros((blk, 1), F32),) * nh)
        for i in range(nh):
            kmax_ref[i:i + 1, :] = jnp.sqrt(jnp.max(best[i], axis=0, keepdims=True)) * BOUND_SLACK

    dot_bound = [jnp.sqrt(head_sq_norm(qs[i].astype(F32), i)) * kmax_ref[i:i + 1, :] for i in range(nh)]

    def tile(kb, state, mask):
        start = pl.multiple_of(kb * blk, blk)
        k = k_ref[pl.ds(start, blk), :]
        v = v_ref[pl.ds(start, blk), :]
        ck = ck_ref[0, 0, :, pl.ds(start, blk)]
        logits = [_dot_nt(qs[i], k[:, _lane_group(i, d)]) + (cq[:, i:i + 1] - ck[i:i + 1, :])
                  for i in range(nh)]
        if mask is not None:
            logits = [jnp.where(mask, logit, NEG_BIG) for logit in logits]
        m_new = [jnp.maximum(state[3 * i], jnp.max(logits[i], axis=-1, keepdims=True)) for i in range(nh)]
        ps = [jnp.exp(logits[i] - m_new[i]) for i in range(nh)]
        pvs = [_dot(ps[i].astype(BF16), v[:, _lane_group(i, d)]) for i in range(nh)]
        new = []
        for i in range(nh):
            alpha = jnp.exp(state[3 * i] - m_new[i])
            new += [m_new[i], alpha * state[3 * i + 1] + jnp.sum(ps[i], axis=-1, keepdims=True),
                    alpha * state[3 * i + 2] + pvs[i]]
        nxt = pl.multiple_of(jnp.maximum(kb - 1, 0) * blk, blk)
        ck_next = ck_ref[0, 0, :, pl.ds(nxt, blk)][:, blk - 1:blk]
        worst = [dot_bound[i] + (cq[:, i:i + 1] - ck_next[i:i + 1, :]) - m_new[i] for i in range(nh)]
        more = (jnp.max(functools.reduce(jnp.maximum, worst)) >= F32_EXP_UNDERFLOW - 1.0).astype(jnp.int32)
        return more, tuple(new)

    init = (jnp.full((blk, 1), NEG_BIG, F32), jnp.zeros((blk, 1), F32), jnp.zeros((blk, LANES), F32)) * nh
    more, state = tile(qi, init, causal)

    def body(loop):
        i, _, st = loop
        more, st = tile(qi - 1 - i, st, None)
        return i + 1, more, st

    _, _, state = lax.while_loop(lambda loop: jnp.logical_and(loop[0] < qi, loop[1] > 0), body,
                                 (jnp.int32(0), more, state))
    o_ref[...] = _select_head_lanes([state[3 * i + 2] / state[3 * i + 1] for i in range(nh)], d)


def _prompt_attention(kernel, q16, kv16, *, batch, extra=()):
    n, d_mix = q16.shape
    s = n // batch
    blk, width = ATT_BLOCK, ATT_LANE_GROUPS * LANES
    assert s % blk == 0 and d_mix % width == 0
    groups, nq = d_mix // width, s // blk
    qspec = pl.BlockSpec((blk, width), lambda bi, g, qi: (bi * nq + qi, g))
    kspec = pl.BlockSpec((s, width), lambda bi, g, qi: (bi, g))
    vspec = pl.BlockSpec((s, width), lambda bi, g, qi: (bi, groups + g))
    in_specs = [qspec, kspec, vspec]
    scratch = []
    if extra:
        per = extra[0].shape[-1]
        in_specs += [pl.BlockSpec((1, 1, blk, per), lambda bi, g, qi: (bi, g, qi, 0)),
                     pl.BlockSpec((1, 1, per, s), lambda bi, g, qi: (bi, g, 0, 0))]
        scratch = [pltpu.VMEM((per, 1), F32)]
    return pl.pallas_call(
        functools.partial(kernel, blk=blk, d=HEAD_DIM),
        grid=(batch, groups, nq),
        in_specs=in_specs,
        out_specs=qspec,
        out_shape=jax.ShapeDtypeStruct((n, d_mix), F32),
        scratch_shapes=scratch,
        compiler_params=pltpu.CompilerParams(
            dimension_semantics=("parallel", "parallel", "arbitrary"),
            vmem_limit_bytes=VMEM_LIMIT_BYTES),
        name=kernel.__name__.strip("_"),
    )(q16, kv16, kv16, *extra)


def _split3_f32(x):
    hi = x.astype(BF16).astype(F32)
    r = x - hi
    mid = r.astype(BF16).astype(F32)
    return hi, mid, (r - mid).astype(BF16).astype(F32)


def _sample_kernel(pt_ref, qs_ref, qf_ref, kvs_new_ref, kvf_new_ref, lft_new_ref, *refs,
                   n_pages_step, page, t_new, n_heads, d_mix):
    npg = n_pages_step
    sb_pages = refs[0:npg]
    fox_pages = refs[npg:2 * npg]
    lf_pages = refs[2 * npg:3 * npg]
    os_ref, of_ref = refs[3 * npg:3 * npg + 2]
    qs_s, qf_s, carry_s, acc_s, carry_f, m_f, l_f, acc_f = refs[3 * npg + 2:]
    step = pl.program_id(1)
    n_rows = n_heads * t_new
    d = d_mix // n_heads

    def block_diag_queries(q):
        reps = jnp.concatenate([q] * n_heads, axis=0)
        r = lax.broadcasted_iota(jnp.int32, (n_rows, d_mix), 0)
        c = lax.broadcasted_iota(jnp.int32, (n_rows, d_mix), 1)
        return jnp.where(r // t_new == c // d, reps, 0.0)

    def rows_from_heads(a):
        return jnp.concatenate([jnp.broadcast_to(a[h:h + 1], (t_new, a.shape[1])) for h in range(n_heads)],
                               axis=0)

    j_idx = lax.broadcasted_iota(jnp.int32, (page, page), 0)
    s_idx = lax.broadcasted_iota(jnp.int32, (page, page), 1)
    neg_suffix = jnp.where(j_idx >= s_idx, -1.0, 0.0).astype(BF16)
    after_m = jnp.where(j_idx > s_idx, 1.0, 0.0)
    row_t = lax.broadcasted_iota(jnp.int32, (n_rows, page), 0) % t_new
    key = lax.broadcasted_iota(jnp.int32, (n_rows, page), 1)

    def attend(blocks, is_new):
        mask_s = key < row_t
        mask_f = key <= row_t
        zs = [blk[0](qs_s[...]) for blk in blocks]
        dots_f = [blk[2](qf_s[...]) for blk in blocks]
        sps = [_softplus(z) for z in zs]
        if is_new:
            sps = [jnp.where(mask_s, sp, 0.0) for sp in sps]
        parts = [_split2(sp) for sp in sps]
        tails = [_dot(hi, neg_suffix) + _dot(lo, neg_suffix) for hi, lo in parts]
        carry = carry_s[...]
        ws = []
        for z, tail in zip(zs, tails):
            w = jnp.exp(z + tail + carry)
            ws.append(jnp.where(mask_s, w, 0.0) if is_new else w)
            carry = carry + tail[:, 0:1]
        carry_s[...] = carry
        carry = carry_f[...]
        logits = []
        for dot_f, blk in zip(dots_f, blocks):
            lft = blk[4]
            hi, mid, lo = _split3_f32(lft)
            after_h = _dot(hi, after_m) + _dot(mid, after_m) + _dot(lo, after_m)
            after = rows_from_heads(after_h)
            total = rows_from_heads(after_h[:, 0:1] + lft[:, 0:1])
            if is_new:
                carry = -jnp.sum(jnp.where(key == row_t, after, 0.0), axis=-1, keepdims=True)
            logit = dot_f + after + carry
            logits.append(jnp.where(mask_f, logit, NEG_BIG) if is_new else logit)
            carry = carry + total
        carry_f[...] = carry
        top = logits[0]
        for logit in logits[1:]:
            top = jnp.maximum(top, logit)
        m_old = m_f[...]
        m_new = jnp.maximum(m_old, jnp.max(top, axis=-1, keepdims=True))
        alpha = jnp.exp(m_old - m_new)
        ps = [jnp.exp(logit - m_new) for logit in logits]
        pv_s = [blk[1](w) for blk, w in zip(blocks, ws)]
        pv_f = [blk[3](p) for blk, p in zip(blocks, ps)]
        acc_s[...] += functools.reduce(lambda a, b: a + b, pv_s)
        l_f[...] = alpha * l_f[...] + jnp.sum(functools.reduce(lambda a, b: a + b, ps), axis=-1, keepdims=True)
        acc_f[...] = alpha * acc_f[...] + functools.reduce(lambda a, b: a + b, pv_f)
        m_f[...] = m_new

    @pl.when(step == 0)
    def _():
        qs_s[...] = block_diag_queries(qs_ref[...])
        qf_s[...] = block_diag_queries(qf_ref[...])
        carry_s[...] = jnp.zeros_like(carry_s)
        acc_s[...] = jnp.zeros_like(acc_s)
        carry_f[...] = jnp.zeros_like(carry_f)
        m_f[...] = jnp.full_like(m_f, NEG_BIG)
        l_f[...] = jnp.zeros_like(l_f)
        acc_f[...] = jnp.zeros_like(acc_f)
        pad = lambda a: jnp.concatenate([a, jnp.zeros((page - t_new, a.shape[1]), F32)], axis=0)
        kv_s, kv_f = pad(kvs_new_ref[...]), pad(kvf_new_ref[...])
        attend([(lambda q: _dot_nt(q, kv_s[:, :d_mix]), lambda w: _dot(w, kv_s[:, d_mix:]),
                 lambda q: _dot_nt(q, kv_f[:, :d_mix]), lambda w: _dot(w, kv_f[:, d_mix:]),
                 lft_new_ref[0])], True)

    def page_block(i):
        ks, vs = sb_pages[i].at[0, 0:d_mix, :], sb_pages[i].at[0, d_mix:2 * d_mix, :]
        kf, vf = fox_pages[i].at[0, 0:d_mix, :], fox_pages[i].at[0, d_mix:2 * d_mix, :]
        return (lambda q: _dot(q, ks[...]), lambda w: _dot_nt(w, vs[...]),
                lambda q: _dot(q, kf[...]), lambda w: _dot_nt(w, vf[...]), lf_pages[i][0])

    attend([page_block(i) for i in range(npg)], False)

    @pl.when(step == pl.num_programs(1) - 1)
    def _():
        def head_diag(acc):
            out = jnp.zeros((t_new, d_mix), F32)
            c = lax.broadcasted_iota(jnp.int32, (t_new, d_mix), 1)
            for h in range(n_heads):
                out = jnp.where(c // d == h, acc[h * t_new:(h + 1) * t_new], out)
            return out
        os_ref[...] = head_diag(acc_s[...])
        of_ref[...] = head_diag(acc_f[...] / l_f[...])


def _sample_attend(page_table, q_sb, q_fox, kv_sb_new, kv_fox_new, lft_new, pool_sb, pool_fox, pool_lf,
                   *, t_new, n_heads):
    n_seq, n_pages = page_table.shape
    page = pool_sb.shape[2]
    d_mix = q_sb.shape[1]
    npg = PAGES_PER_STEP
    assert n_pages % npg == 0
    n_rows = n_heads * t_new
    seq = lambda width: pl.BlockSpec((t_new, width), lambda b, j, pt: (b, 0))

    def page_spec(rows, i):
        return pl.BlockSpec((1, rows, page), lambda b, j, pt: (pt[b, n_pages - 1 - (j * npg + i)], 0, 0))

    in_specs = [seq(d_mix), seq(d_mix), seq(2 * d_mix), seq(2 * d_mix),
                pl.BlockSpec((1, n_heads, page), lambda b, j, pt: (b, 0, 0))]
    in_specs += [page_spec(2 * d_mix, i) for i in range(npg)]
    in_specs += [page_spec(2 * d_mix, i) for i in range(npg)]
    in_specs += [page_spec(n_heads, i) for i in range(npg)]
    scratch = [pltpu.VMEM((n_rows, d_mix), F32), pltpu.VMEM((n_rows, d_mix), F32),
               pltpu.VMEM((n_rows, 1), F32), pltpu.VMEM((n_rows, d_mix), F32),
               pltpu.VMEM((n_rows, 1), F32), pltpu.VMEM((n_rows, 1), F32),
               pltpu.VMEM((n_rows, 1), F32), pltpu.VMEM((n_rows, d_mix), F32)]
    grid_spec = pltpu.PrefetchScalarGridSpec(
        num_scalar_prefetch=1,
        grid=(n_seq, n_pages // npg),
        in_specs=in_specs,
        out_specs=[seq(d_mix), seq(d_mix)],
        scratch_shapes=scratch)
    return pl.pallas_call(
        functools.partial(_sample_kernel, n_pages_step=npg, page=page, t_new=t_new, n_heads=n_heads,
                          d_mix=d_mix),
        grid_spec=grid_spec,
        out_shape=[jax.ShapeDtypeStruct((n_seq * t_new, d_mix), F32)] * 2,
        compiler_params=pltpu.CompilerParams(dimension_semantics=("parallel", "arbitrary"),
                                             vmem_limit_bytes=VMEM_LIMIT_BYTES),
        name="sample_attend",
    )(page_table, q_sb, q_fox, kv_sb_new, kv_fox_new, lft_new,
      *([pool_sb] * npg), *([pool_fox] * npg), *([pool_lf] * npg))


def _merge_kernel(h_ref, os_ref, of_ref, zs_ref, zf_ref, p_ref, gs_ref, gf_ref, wo_ref, wp_ref, gp_ref,
                  wg_ref, gfin_ref, y_ref, *, head_dim):
    def head_norm_gate(o, g, z):
        width = o.shape[-1]
        r = lax.broadcasted_iota(jnp.int32, (width, width), 0) // head_dim
        c = lax.broadcasted_iota(jnp.int32, (width, width), 1) // head_dim
        same_head = jnp.where(r == c, 1.0, 0.0).astype(BF16)
        hi, lo = _split2(o * o)
        ms = (_dot(hi, same_head) + _dot(lo, same_head)) * (1.0 / head_dim)
        y = o * lax.rsqrt(ms + EPS) * g
        return y * (z * jax.nn.sigmoid(z))

    def rms(x, g):
        return x * lax.rsqrt(jnp.mean(x * x, axis=-1, keepdims=True) + EPS) * g

    a = head_norm_gate(os_ref[...], gs_ref[...], zs_ref[...])
    b = head_norm_gate(of_ref[...], gf_ref[...], zf_ref[...])
    d_sb = a.shape[-1]
    h = h_ref[...] + _dot(a.astype(BF16), wo_ref[:d_sb, :]) + _dot(b.astype(BF16), wo_ref[d_sb:, :])
    gate = jax.nn.sigmoid(_dot(rms(h, gp_ref[...]).astype(BF16), wg_ref[...]))
    h = h + _dot(p_ref[...].astype(BF16), wp_ref[...]) * gate
    y_ref[...] = rms(h, gfin_ref[...])


def _merge(h, o_sb, o_fox, z_sb, z_fox, p, g_out_sb, g_out_fox, w_out, w_ple, g_ple, w_gate, g_final):
    n, d_model = h.shape
    tm = ROW_BLOCK
    assert n % tm == 0
    row = lambda a: pl.BlockSpec((tm, a.shape[1]), lambda i: (i, 0))
    full = lambda a: pl.BlockSpec(a.shape, lambda i: (0,) * a.ndim)
    rows = (h, o_sb, o_fox, z_sb, z_fox, p)
    params = (g_out_sb, g_out_fox, w_out, w_ple, g_ple, w_gate, g_final)
    return pl.pallas_call(
        functools.partial(_merge_kernel, head_dim=HEAD_DIM),
        grid=(n // tm,),
        in_specs=[row(a) for a in rows] + [full(a) for a in params],
        out_specs=pl.BlockSpec((tm, d_model), lambda i: (i, 0)),
        out_shape=jax.ShapeDtypeStruct((n, d_model), F32),
        compiler_params=pltpu.CompilerParams(dimension_semantics=("parallel",),
                                             vmem_limit_bytes=VMEM_LIMIT_BYTES),
        name="merge",
    )(*rows, *params)


def kernel(x_prompt, x_sample, cache_sb_kv, cache_fox_kv, cache_fox_logf, page_table, p_prompt, p_sample,
           g_norm, w_in, b_f, g_out_sb, g_out_fox, w_out, w_ple, g_ple, w_ple_gate, g_final):
    depth = g_norm.shape[0]
    assert depth == 1, "single-layer decoder step"
    b, s, d_model = x_prompt.shape
    bd, t_new, _ = x_sample.shape
    h_sb, h_fox = cache_sb_kv.shape[4], cache_fox_kv.shape[4]
    d_sb, d_fox = h_sb * HEAD_DIM, h_fox * HEAD_DIM
    assert d_sb == d_fox and h_sb == h_fox
    n_main = 4 * d_sb + 4 * d_fox
    scale = HEAD_DIM ** -0.5
    row2 = lambda a: a.reshape(1, -1)

    w_main = w_in[0, :, :n_main].astype(BF16)
    w_f = jnp.pad(w_in[0, :, n_main:], ((0, 0), (0, LANES - h_fox))).astype(BF16)
    proj = functools.partial(_project, g_norm=row2(g_norm[0]), w_main=w_main, w_f=w_f, b_f=row2(b_f[0]),
                             d_sb=d_sb, d_fox=d_fox, h_fox=h_fox, scale=scale)
    merge = functools.partial(
        _merge, g_out_sb=row2(g_out_sb[0]), g_out_fox=row2(g_out_fox[0]), w_out=w_out[0].astype(BF16),
        w_ple=w_ple[0].astype(BF16), g_ple=row2(g_ple[0]), w_gate=w_ple_gate[0].astype(BF16),
        g_final=row2(g_final))

    xp = x_prompt.reshape(b * s, d_model)
    kvs, zs, kvf, zf, lf, qs16, kvs16, qf16, kvf16 = proj(xp)
    o_s = _prompt_attention(_sb_prompt_kernel, qs16, kvs16, batch=b)
    c_t = _cumsum_rows(jnp.transpose(lf.reshape(b, s, h_fox), (0, 2, 1)).reshape(b * h_fox, s))
    per = ATT_LANE_GROUPS * LANES // HEAD_DIM
    ck = c_t.reshape(b, h_fox // per, per, s)
    o_f = _prompt_attention(_fox_prompt_kernel, qf16, kvf16, batch=b,
                            extra=(jnp.transpose(ck, (0, 1, 3, 2)), ck))
    y_prompt = merge(xp, o_s, o_f, zs, zf, p_prompt[0].reshape(b * s, -1))

    xs = x_sample.reshape(bd * t_new, d_model)
    kvs2, zs2, kvf2, zf2, lf2, qs2, _, qf2, _ = proj(xs)
    n_pool, page = cache_sb_kv.shape[1], cache_sb_kv.shape[2]
    pool_view = lambda c: jnp.transpose(c[0], (0, 2, 3, 4, 1)).reshape(n_pool, -1, page)
    lft_new = jnp.pad(jnp.transpose(lf2.reshape(bd, t_new, h_fox), (0, 2, 1)),
                      ((0, 0), (0, 0), (0, page - t_new)))
    o_s2, o_f2 = _sample_attend(
        page_table, qs2.astype(F32), qf2.astype(F32), kvs2, kvf2, lft_new,
        pool_view(cache_sb_kv), pool_view(cache_fox_kv), jnp.transpose(cache_fox_logf[0], (0, 2, 1)),
        t_new=t_new, n_heads=h_sb)
    y_sample = merge(xs, o_s2, o_f2, zs2, zf2, p_sample[0].reshape(bd * t_new, -1))

    return (y_prompt.reshape(b, s, d_model), y_sample.reshape(bd, t_new, d_model),
            kvs.reshape(1, b, s, 2, h_sb, HEAD_DIM), kvf.reshape(1, b, s, 2, h_fox, HEAD_DIM),
            lf.reshape(1, b, s, h_fox),
            kvs2.reshape(1, bd, t_new, 2, h_sb, HEAD_DIM), kvf2.reshape(1, bd, t_new, 2, h_fox, HEAD_DIM),
            lf2.reshape(1, bd, t_new, h_fox))
```

```python
import functools

import jax
import jax.numpy as jnp
from jax import lax
from jax.experimental import pallas as pl
from jax.experimental.pallas import tpu as pltpu

F32 = jnp.float32
BF16 = jnp.bfloat16

HEAD_DIM = 64
EPS = 1e-6
NEG_BIG = -1e30
F32_EXP_UNDERFLOW = -104.0
BOUND_SLACK = 1.0 + 2.0 ** -10

LANES = 128
MXU_DIM = 256
VMEM_LIMIT_BYTES = 56 * 1024 * 1024

ROW_BLOCK = 256
ATT_BLOCK = MXU_DIM
ATT_LANE_GROUPS = 2
PAGES_PER_STEP = 16


def _softplus(z):
    return jnp.maximum(z, 0.0) + jnp.log(1.0 + jnp.exp(-jnp.abs(z)))


def _split2(x):
    hi = x.astype(BF16)
    lo = (x - hi.astype(F32)).astype(BF16)
    return hi, lo


def _split3(x):
    hi = x.astype(BF16)
    r = x - hi.astype(F32)
    mid = r.astype(BF16)
    lo = (r - mid.astype(F32)).astype(BF16)
    return hi, mid, lo


def _dot(a, b):
    return jnp.dot(a, b, preferred_element_type=F32)


def _dot_nt(a, b):
    return lax.dot_general(a, b, (((1,), (1,)), ((), ())), preferred_element_type=F32)


def _project_kernel(x_ref, g_ref, w_ref, wf_ref, bf_ref,
                    kvs_ref, zs_ref, kvf_ref, zf_ref, lf_ref, qs16_ref, kvs16_ref, qf16_ref, kvf16_ref,
                    *, d_sb, d_fox, h_fox, scale, kv_token_minor):
    x = x_ref[...]
    ms = jnp.mean(x * x, axis=-1, keepdims=True)
    u = (x * lax.rsqrt(ms + EPS) * g_ref[...]).astype(BF16)
    o = 0
    for q16_ref, kv_ref, kv16_ref, z_ref, width in ((qs16_ref, kvs_ref, kvs16_ref, zs_ref, d_sb),
                                                    (qf16_ref, kvf_ref, kvf16_ref, zf_ref, d_fox)):
        q16_ref[...] = (_dot(u, w_ref[:, o:o + width]) * scale).astype(BF16)
        kv = _dot(u, w_ref[:, o + width:o + 3 * width])
        if kv_token_minor:
            kv_ref[0] = kv.T
        else:
            kv_ref[...] = kv
        kv16_ref[...] = kv.astype(BF16)
        z_ref[...] = _dot(u, w_ref[:, o + 3 * width:o + 4 * width])
        o += 4 * width
    f = _dot(u, wf_ref[...])[:, :h_fox] + bf_ref[...]
    lf_ref[...] = jnp.minimum(f, 0.0) - jnp.log(1.0 + jnp.exp(-jnp.abs(f)))


def _project(x, g_norm, w_main, w_f, b_f, *, d_sb, d_fox, h_fox, scale, token_minor_batch=None):
    n, d_model = x.shape
    tm = ROW_BLOCK
    assert n % tm == 0
    row = lambda width: pl.BlockSpec((tm, width), lambda i: (i, 0))
    full = lambda a: pl.BlockSpec(a.shape, lambda i: (0,) * a.ndim)
    outs = ((2 * d_sb, F32), (d_sb, F32), (2 * d_fox, F32), (d_fox, F32), (h_fox, F32),
            (d_sb, BF16), (2 * d_sb, BF16), (d_fox, BF16), (2 * d_fox, BF16))
    out_specs = [row(w) for w, _ in outs]
    out_shape = [jax.ShapeDtypeStruct((n, w), dt) for w, dt in outs]
    if token_minor_batch is not None:
        seq = n // token_minor_batch
        assert seq % tm == 0
        per_seq = seq // tm
        for j in (0, 2):
            out_specs[j] = pl.BlockSpec((1, outs[j][0], tm), lambda i: (i // per_seq, 0, i % per_seq))
            out_shape[j] = jax.ShapeDtypeStruct((token_minor_batch, outs[j][0], seq), F32)
    return pl.pallas_call(
        functools.partial(_project_kernel, d_sb=d_sb, d_fox=d_fox, h_fox=h_fox, scale=scale,
                          kv_token_minor=token_minor_batch is not None),
        grid=(n // tm,),
        in_specs=[row(d_model), full(g_norm), full(w_main), full(w_f), full(b_f)],
        out_specs=out_specs,
        out_shape=out_shape,
        compiler_params=pltpu.CompilerParams(dimension_semantics=("parallel",),
                                             vmem_limit_bytes=VMEM_LIMIT_BYTES),
        name="project",
    )(x, g_norm, w_main, w_f, b_f)


def _cumsum_kernel(x_ref, o_ref, *, chunk):
    rows, n = x_ref.shape
    r = lax.broadcasted_iota(jnp.int32, (chunk, chunk), 0)
    c = lax.broadcasted_iota(jnp.int32, (chunk, chunk), 1)
    upper = jnp.where(r <= c, 1.0, 0.0).astype(BF16)
    carry = jnp.zeros((rows, 1), F32)
    for i in range(n // chunk):
        hi, mid, lo = _split3(x_ref[:, i * chunk:(i + 1) * chunk])
        local = _dot(hi, upper) + _dot(mid, upper) + _dot(lo, upper)
        o_ref[:, i * chunk:(i + 1) * chunk] = local + carry
        carry = carry + local[:, chunk - 1:chunk]


def _cumsum_rows(x):
    return pl.pallas_call(
        functools.partial(_cumsum_kernel, chunk=MXU_DIM),
        out_shape=jax.ShapeDtypeStruct(x.shape, F32),
        name="cumsum_rows",
    )(x)


def _suffix_matrix(n, strict):
    j = lax.broadcasted_iota(jnp.int32, (n, n), 0)
    s = lax.broadcasted_iota(jnp.int32, (n, n), 1)
    keep = (j > s) if strict else (j >= s)
    return jnp.where(keep, -1.0, 0.0).astype(BF16)


def _per_head_queries(q_ref, d):
    q = q_ref[...].astype(F32)
    out = []
    for g in range(q.shape[1] // LANES):
        qg = q[:, g * LANES:(g + 1) * LANES]
        lane = lax.broadcasted_iota(jnp.int32, qg.shape, 1)
        out += [jnp.where(lane // d == i, qg, 0.0).astype(BF16) for i in range(LANES // d)]
    return out


def _lane_group(i, d):
    g = i // (LANES // d)
    return slice(g * LANES, (g + 1) * LANES)


def _select_head_lanes(accs, d):
    per = LANES // d
    lane = lax.broadcasted_iota(jnp.int32, accs[0].shape, 1)
    groups = []
    for g in range(len(accs) // per):
        out = accs[g * per]
        for i in range(1, per):
            out = jnp.where(lane // d == i, accs[g * per + i], out)
        groups.append(out)
    return jnp.concatenate(groups, axis=1)


def _sb_prompt_kernel(q_ref, k_ref, v_ref, o_ref, *, blk, d):
    qi = pl.program_id(2)
    qs = _per_head_queries(q_ref, d)
    nh = len(qs)
    neg_suffix = _suffix_matrix(blk, strict=False)
    t_idx = lax.broadcasted_iota(jnp.int32, (blk, blk), 0)
    s_idx = lax.broadcasted_iota(jnp.int32, (blk, blk), 1)
    causal = s_idx < t_idx

    def tile(kb, state, mask):
        start = pl.multiple_of(kb * blk, blk)
        k = k_ref[pl.ds(start, blk), :]
        v = v_ref[pl.ds(start, blk), :]
        zs = [_dot_nt(qs[i], k[:, _lane_group(i, d)]) for i in range(nh)]
        sps = [_softplus(z) for z in zs]
        if mask is not None:
            sps = [jnp.where(mask, sp, 0.0) for sp in sps]
        parts = [_split2(sp) for sp in sps]
        tails = [_dot(hi, neg_suffix) + _dot(lo, neg_suffix) for hi, lo in parts]
        ws = [jnp.exp(zs[i] + tails[i] + state[2 * i]) for i in range(nh)]
        if mask is not None:
            ws = [jnp.where(mask, w, 0.0) for w in ws]
        pvs = [_dot(ws[i].astype(BF16), v[:, _lane_group(i, d)]) for i in range(nh)]
        new = []
        for i in range(nh):
            new += [state[2 * i] + tails[i][:, 0:1], state[2 * i + 1] + pvs[i]]
        top = functools.reduce(jnp.maximum, new[0::2])
        more = (jnp.max(top) >= F32_EXP_UNDERFLOW).astype(jnp.int32)
        return more, tuple(new)

    init = (jnp.zeros((blk, 1), F32), jnp.zeros((blk, LANES), F32)) * nh
    more, state = tile(qi, init, causal)

    def body(loop):
        i, _, st = loop
        more, st = tile(qi - 1 - i, st, None)
        return i + 1, more, st

    _, _, state = lax.while_loop(lambda loop: jnp.logical_and(loop[0] < qi, loop[1] > 0), body,
                                 (jnp.int32(0), more, state))
    o_ref[...] = _select_head_lanes([state[2 * i + 1] for i in range(nh)], d)


def _fox_prompt_kernel(q_ref, k_ref, v_ref, cq_ref, ck_ref, o_ref, kmax_ref, *, blk, d):
    qi = pl.program_id(2)
    qs = _per_head_queries(q_ref, d)
    nh = len(qs)
    cq = cq_ref[0, 0]
    t_idx = lax.broadcasted_iota(jnp.int32, (blk, blk), 0)
    s_idx = lax.broadcasted_iota(jnp.int32, (blk, blk), 1)
    causal = s_idx <= t_idx
    head_lane = lax.broadcasted_iota(jnp.int32, (blk, LANES), 1) // d

    def head_sq_norm(x, i):
        return jnp.sum(jnp.where(head_lane == i % (LANES // d), x * x, 0.0), axis=-1, keepdims=True)

    @pl.when(qi == 0)
    def _():
        def chunk(c, best):
            kc = k_ref[pl.ds(pl.multiple_of(c * blk, blk), blk), :].astype(F32)
            return tuple(jnp.maximum(best[i], head_sq_norm(kc[:, _lane_group(i, d)], i)) for i in range(nh))

        best = lax.fori_loop(0, k_ref.shape[0] // blk, chunk, (jnp.zeros((blk, 1), F32),) * nh)
        for i in range(nh):
            kmax_ref[i:i + 1, :] = jnp.sqrt(jnp.max(best[i], axis=0, keepdims=True)) * BOUND_SLACK

    dot_bound = [jnp.sqrt(head_sq_norm(qs[i].astype(F32), i)) * kmax_ref[i:i + 1, :] for i in range(nh)]

    def tile(kb, state, mask):
        start = pl.multiple_of(kb * blk, blk)
        k = k_ref[pl.ds(start, blk), :]
        v = v_ref[pl.ds(start, blk), :]
        ck = ck_ref[0, 0, :, pl.ds(start, blk)]
        logits = [_dot_nt(qs[i], k[:, _lane_group(i, d)]) + (cq[:, i:i + 1] - ck[i:i + 1, :])
                  for i in range(nh)]
        if mask is not None:
            logits = [jnp.where(mask, logit, NEG_BIG) for logit in logits]
        m_new = [jnp.maximum(state[3 * i], jnp.max(logits[i], axis=-1, keepdims=True)) for i in range(nh)]
        ps = [jnp.exp(logits[i] - m_new[i]) for i in range(nh)]
        pvs = [_dot(ps[i].astype(BF16), v[:, _lane_group(i, d)]) for i in range(nh)]
        new = []
        for i in range(nh):
            alpha = jnp.exp(state[3 * i] - m_new[i])
            new += [m_new[i], alpha * state[3 * i + 1] + jnp.sum(ps[i], axis=-1, keepdims=True),
                    alpha * state[3 * i + 2] + pvs[i]]
        nxt = pl.multiple_of(jnp.maximum(kb - 1, 0) * blk, blk)
        ck_next = ck_ref[0, 0, :, pl.ds(nxt, blk)][:, blk - 1:blk]
        worst = [dot_bound[i] + (cq[:, i:i + 1] - ck_next[i:i + 1, :]) - m_new[i] for i in range(nh)]
        more = (jnp.max(functools.reduce(jnp.maximum, worst)) >= F32_EXP_UNDERFLOW - 1.0).astype(jnp.int32)
        return more, tuple(new)

    init = (jnp.full((blk, 1), NEG_BIG, F32), jnp.zeros((blk, 1), F32), jnp.zeros((blk, LANES), F32)) * nh
    more, state = tile(qi, init, causal)

    def body(loop):
        i, _, st = loop
        more, st = tile(qi - 1 - i, st, None)
        return i + 1, more, st

    _, _, state = lax.while_loop(lambda loop: jnp.logical_and(loop[0] < qi, loop[1] > 0), body,
                                 (jnp.int32(0), more, state))
    o_ref[...] = _select_head_lanes([state[3 * i + 2] / state[3 * i + 1] for i in range(nh)], d)


def _prompt_attention(kernel, q16, kv16, *, batch, extra=()):
    n, d_mix = q16.shape
    s = n // batch
    blk, width = ATT_BLOCK, ATT_LANE_GROUPS * LANES
    assert s % blk == 0 and d_mix % width == 0
    groups, nq = d_mix // width, s // blk
    qspec = pl.BlockSpec((blk, width), lambda bi, g, qi: (bi * nq + qi, g))
    kspec = pl.BlockSpec((s, width), lambda bi, g, qi: (bi, g))
    vspec = pl.BlockSpec((s, width), lambda bi, g, qi: (bi, groups + g))
    in_specs = [qspec, kspec, vspec]
    scratch = []
    if extra:
        per = extra[0].shape[-1]
        in_specs += [pl.BlockSpec((1, 1, blk, per), lambda bi, g, qi: (bi, g, qi, 0)),
                     pl.BlockSpec((1, 1, per, s), lambda bi, g, qi: (bi, g, 0, 0))]
        scratch = [pltpu.VMEM((per, 1), F32)]
    return pl.pallas_call(
        functools.partial(kernel, blk=blk, d=HEAD_DIM),
        grid=(batch, groups, nq),
        in_specs=in_specs,
        out_specs=qspec,
        out_shape=jax.ShapeDtypeStruct((n, d_mix), F32),
        scratch_shapes=scratch,
        compiler_params=pltpu.CompilerParams(
            dimension_semantics=("parallel", "parallel", "arbitrary"),
            vmem_limit_bytes=VMEM_LIMIT_BYTES),
        name=kernel.__name__.strip("_"),
    )(q16, kv16, kv16, *extra)


def _split3_f32(x):
    hi = x.astype(BF16).astype(F32)
    r = x - hi
    mid = r.astype(BF16).astype(F32)
    return hi, mid, (r - mid).astype(BF16).astype(F32)


def _sample_kernel(pt_ref, qs_ref, qf_ref, kvs_new_ref, kvf_new_ref, lft_new_ref, *refs,
                   n_pages_step, page, t_new, n_heads, d_mix):
    npg = n_pages_step
    sb_pages = refs[0:npg]
    fox_pages = refs[npg:2 * npg]
    lf_pages = refs[2 * npg:3 * npg]
    os_ref, of_ref = refs[3 * npg:3 * npg + 2]
    qs_s, qf_s, carry_s, acc_s, carry_f, m_f, l_f, acc_f = refs[3 * npg + 2:]
    step = pl.program_id(1)
    n_rows = n_heads * t_new
    d = d_mix // n_heads

    def block_diag_queries(q):
        reps = jnp.concatenate([q] * n_heads, axis=0)
        r = lax.broadcasted_iota(jnp.int32, (n_rows, d_mix), 0)
        c = lax.broadcasted_iota(jnp.int32, (n_rows, d_mix), 1)
        return jnp.where(r // t_new == c // d, reps, 0.0)

    def rows_from_heads(a):
        return jnp.concatenate([jnp.broadcast_to(a[h:h + 1], (t_new, a.shape[1])) for h in range(n_heads)],
                               axis=0)

    j_idx = lax.broadcasted_iota(jnp.int32, (page, page), 0)
    s_idx = lax.broadcasted_iota(jnp.int32, (page, page), 1)
    neg_suffix = jnp.where(j_idx >= s_idx, -1.0, 0.0).astype(BF16)
    after_m = jnp.where(j_idx > s_idx, 1.0, 0.0)
    row_t = lax.broadcasted_iota(jnp.int32, (n_rows, page), 0) % t_new
    key = lax.broadcasted_iota(jnp.int32, (n_rows, page), 1)

    def attend(blocks, is_new):
        mask_s = key < row_t
        mask_f = key <= row_t
        zs = [blk[0](qs_s[...]) for blk in blocks]
        dots_f = [blk[2](qf_s[...]) for blk in blocks]
        sps = [_softplus(z) for z in zs]
        if is_new:
            sps = [jnp.where(mask_s, sp, 0.0) for sp in sps]
        parts = [_split2(sp) for sp in sps]
        tails = [_dot(hi, neg_suffix) + _dot(lo, neg_suffix) for hi, lo in parts]
        carry = carry_s[...]
        ws = []
        for z, tail in zip(zs, tails):
            w = jnp.exp(z + tail + carry)
            ws.append(jnp.where(mask_s, w, 0.0) if is_new else w)
            carry = carry + tail[:, 0:1]
        carry_s[...] = carry
        carry = carry_f[...]
        logits = []
        for dot_f, blk in zip(dots_f, blocks):
            lft = blk[4]
            hi, mid, lo = _split3_f32(lft)
            after_h = _dot(hi, after_m) + _dot(mid, after_m) + _dot(lo, after_m)
            after = rows_from_heads(after_h)
            total = rows_from_heads(after_h[:, 0:1] + lft[:, 0:1])
            if is_new:
                carry = -jnp.sum(jnp.where(key == row_t, after, 0.0), axis=-1, keepdims=True)
            logit = dot_f + after + carry
            logits.append(jnp.where(mask_f, logit, NEG_BIG) if is_new else logit)
            carry = carry + total
        carry_f[...] = carry
        top = logits[0]
        for logit in logits[1:]:
            top = jnp.maximum(top, logit)
        m_old = m_f[...]
        m_new = jnp.maximum(m_old, jnp.max(top, axis=-1, keepdims=True))
        alpha = jnp.exp(m_old - m_new)
        ps = [jnp.exp(logit - m_new) for logit in logits]
        pv_s = [blk[1](w) for blk, w in zip(blocks, ws)]
        pv_f = [blk[3](p) for blk, p in zip(blocks, ps)]
        acc_s[...] += functools.reduce(lambda a, b: a + b, pv_s)
        l_f[...] = alpha * l_f[...] + jnp.sum(functools.reduce(lambda a, b: a + b, ps), axis=-1, keepdims=True)
        acc_f[...] = alpha * acc_f[...] + functools.reduce(lambda a, b: a + b, pv_f)
        m_f[...] = m_new

    @pl.when(step == 0)
    def _():
        qs_s[...] = block_diag_queries(qs_ref[...])
        qf_s[...] = block_diag_queries(qf_ref[...])
        carry_s[...] = jnp.zeros_like(carry_s)
        acc_s[...] = jnp.zeros_like(acc_s)
        carry_f[...] = jnp.zeros_like(carry_f)
        m_f[...] = jnp.full_like(m_f, NEG_BIG)
        l_f[...] = jnp.zeros_like(l_f)
        acc_f[...] = jnp.zeros_like(acc_f)
        pad = lambda a: jnp.concatenate([a, jnp.zeros((page - t_new, a.shape[1]), F32)], axis=0)
        kv_s, kv_f = pad(kvs_new_ref[...]), pad(kvf_new_ref[...])
        attend([(lambda q: _dot_nt(q, kv_s[:, :d_mix]), lambda w: _dot(w, kv_s[:, d_mix:]),
                 lambda q: _dot_nt(q, kv_f[:, :d_mix]), lambda w: _dot(w, kv_f[:, d_mix:]),
                 lft_new_ref[0])], True)

    def page_block(i):
        ks, vs = sb_pages[i].at[0, 0:d_mix, :], sb_pages[i].at[0, d_mix:2 * d_mix, :]
        kf, vf = fox_pages[i].at[0, 0:d_mix, :], fox_pages[i].at[0, d_mix:2 * d_mix, :]
        return (lambda q: _dot(q, ks[...]), lambda w: _dot_nt(w, vs[...]),
                lambda q: _dot(q, kf[...]), lambda w: _dot_nt(w, vf[...]), lf_pages[i][0])

    attend([page_block(i) for i in range(npg)], False)

    @pl.when(step == pl.num_programs(1) - 1)
    def _():
        def head_diag(acc):
            out = jnp.zeros((t_new, d_mix), F32)
            c = lax.broadcasted_iota(jnp.int32, (t_new, d_mix), 1)
            for h in range(n_heads):
                out = jnp.where(c // d == h, acc[h * t_new:(h + 1) * t_new], out)
            return out
        os_ref[...] = head_diag(acc_s[...])
        of_ref[...] = head_diag(acc_f[...] / l_f[...])


def _sample_attend(page_table, q_sb, q_fox, kv_sb_new, kv_fox_new, lft_new, pool_sb, pool_fox, pool_lf,
                   *, t_new, n_heads):
    n_seq, n_pages = page_table.shape
    page = pool_sb.shape[2]
    d_mix = q_sb.shape[1]
    npg = PAGES_PER_STEP
    assert n_pages % npg == 0
    n_rows = n_heads * t_new
    seq = lambda width: pl.BlockSpec((t_new, width), lambda b, j, pt: (b, 0))

    def page_spec(rows, i):
        return pl.BlockSpec((1, rows, page), lambda b, j, pt: (pt[b, n_pages - 1 - (j * npg + i)], 0, 0))

    in_specs = [seq(d_mix), seq(d_mix), seq(2 * d_mix), seq(2 * d_mix),
                pl.BlockSpec((1, n_heads, page), lambda b, j, pt: (b, 0, 0))]
    in_specs += [page_spec(2 * d_mix, i) for i in range(npg)]
    in_specs += [page_spec(2 * d_mix, i) for i in range(npg)]
    in_specs += [page_spec(n_heads, i) for i in range(npg)]
    scratch = [pltpu.VMEM((n_rows, d_mix), F32), pltpu.VMEM((n_rows, d_mix), F32),
               pltpu.VMEM((n_rows, 1), F32), pltpu.VMEM((n_rows, d_mix), F32),
               pltpu.VMEM((n_rows, 1), F32), pltpu.VMEM((n_rows, 1), F32),
               pltpu.VMEM((n_rows, 1), F32), pltpu.VMEM((n_rows, d_mix), F32)]
    grid_spec = pltpu.PrefetchScalarGridSpec(
        num_scalar_prefetch=1,
        grid=(n_seq, n_pages // npg),
        in_specs=in_specs,
        out_specs=[seq(d_mix), seq(d_mix)],
        scratch_shapes=scratch)
    return pl.pallas_call(
        functools.partial(_sample_kernel, n_pages_step=npg, page=page, t_new=t_new, n_heads=n_heads,
                          d_mix=d_mix),
        grid_spec=grid_spec,
        out_shape=[jax.ShapeDtypeStruct((n_seq * t_new, d_mix), F32)] * 2,
        compiler_params=pltpu.CompilerParams(dimension_semantics=("parallel", "arbitrary"),
                                             vmem_limit_bytes=VMEM_LIMIT_BYTES),
        name="sample_attend",
    )(page_table, q_sb, q_fox, kv_sb_new, kv_fox_new, lft_new,
      *([pool_sb] * npg), *([pool_fox] * npg), *([pool_lf] * npg))


def _merge_kernel(h_ref, os_ref, of_ref, zs_ref, zf_ref, p_ref, gs_ref, gf_ref, wo_ref, wp_ref, gp_ref,
                  wg_ref, gfin_ref, y_ref, *, head_dim):
    def head_norm_gate(o, g, z):
        width = o.shape[-1]
        r = lax.broadcasted_iota(jnp.int32, (width, width), 0) // head_dim
        c = lax.broadcasted_iota(jnp.int32, (width, width), 1) // head_dim
        same_head = jnp.where(r == c, 1.0, 0.0).astype(BF16)
        hi, lo = _split2(o * o)
        ms = (_dot(hi, same_head) + _dot(lo, same_head)) * (1.0 / head_dim)
        y = o * lax.rsqrt(ms + EPS) * g
        return y * (z * jax.nn.sigmoid(z))

    def rms(x, g):
        return x * lax.rsqrt(jnp.mean(x * x, axis=-1, keepdims=True) + EPS) * g

    a = head_norm_gate(os_ref[...], gs_ref[...], zs_ref[...])
    b = head_norm_gate(of_ref[...], gf_ref[...], zf_ref[...])
    d_sb = a.shape[-1]
    h = h_ref[...] + _dot(a.astype(BF16), wo_ref[:d_sb, :]) + _dot(b.astype(BF16), wo_ref[d_sb:, :])
    gate = jax.nn.sigmoid(_dot(rms(h, gp_ref[...]).astype(BF16), wg_ref[...]))
    h = h + _dot(p_ref[...].astype(BF16), wp_ref[...]) * gate
    y_ref[...] = rms(h, gfin_ref[...])


def _merge(h, o_sb, o_fox, z_sb, z_fox, p, g_out_sb, g_out_fox, w_out, w_ple, g_ple, w_gate, g_final):
    n, d_model = h.shape
    tm = ROW_BLOCK
    assert n % tm == 0
    row = lambda a: pl.BlockSpec((tm, a.shape[1]), lambda i: (i, 0))
    full = lambda a: pl.BlockSpec(a.shape, lambda i: (0,) * a.ndim)
    rows = (h, o_sb, o_fox, z_sb, z_fox, p)
    params = (g_out_sb, g_out_fox, w_out, w_ple, g_ple, w_gate, g_final)
    return pl.pallas_call(
        functools.partial(_merge_kernel, head_dim=HEAD_DIM),
        grid=(n // tm,),
        in_specs=[row(a) for a in rows] + [full(a) for a in params],
        out_specs=pl.BlockSpec((tm, d_model), lambda i: (i, 0)),
        out_shape=jax.ShapeDtypeStruct((n, d_model), F32),
        compiler_params=pltpu.CompilerParams(dimension_semantics=("parallel",),
                                             vmem_limit_bytes=VMEM_LIMIT_BYTES),
        name="merge",
    )(*rows, *params)


def kernel(x_prompt, x_sample, cache_sb_kv, cache_fox_kv, cache_fox_logf, page_table, p_prompt, p_sample,
           g_norm, w_in, b_f, g_out_sb, g_out_fox, w_out, w_ple, g_ple, w_ple_gate, g_final):
    depth = g_norm.shape[0]
    assert depth == 1, "single-layer decoder step"
    b, s, d_model = x_prompt.shape
    bd, t_new, _ = x_sample.shape
    h_sb, h_fox = cache_sb_kv.shape[4], cache_fox_kv.shape[4]
    d_sb, d_fox = h_sb * HEAD_DIM, h_fox * HEAD_DIM
    assert d_sb == d_fox and h_sb == h_fox
    n_main = 4 * d_sb + 4 * d_fox
    scale = HEAD_DIM ** -0.5
    row2 = lambda a: a.reshape(1, -1)

    w_main = w_in[0, :, :n_main].astype(BF16)
    w_f = jnp.pad(w_in[0, :, n_main:], ((0, 0), (0, LANES - h_fox))).astype(BF16)
    proj = functools.partial(_project, g_norm=row2(g_norm[0]), w_main=w_main, w_f=w_f, b_f=row2(b_f[0]),
                             d_sb=d_sb, d_fox=d_fox, h_fox=h_fox, scale=scale)
    merge = functools.partial(
        _merge, g_out_sb=row2(g_out_sb[0]), g_out_fox=row2(g_out_fox[0]), w_out=w_out[0].astype(BF16),
        w_ple=w_ple[0].astype(BF16), g_ple=row2(g_ple[0]), w_gate=w_ple_gate[0].astype(BF16),
        g_final=row2(g_final))

    xp = x_prompt.reshape(b * s, d_model)
    kvs_t, zs, kvf_t, zf, lf, qs16, kvs16, qf16, kvf16 = proj(xp, token_minor_batch=b)
    o_s = _prompt_attention(_sb_prompt_kernel, qs16, kvs16, batch=b)
    c_t = _cumsum_rows(jnp.transpose(lf.reshape(b, s, h_fox), (0, 2, 1)).reshape(b * h_fox, s))
    per = ATT_LANE_GROUPS * LANES // HEAD_DIM
    ck = c_t.reshape(b, h_fox // per, per, s)
    o_f = _prompt_attention(_fox_prompt_kernel, qf16, kvf16, batch=b,
                            extra=(jnp.transpose(ck, (0, 1, 3, 2)), ck))
    y_prompt = merge(xp, o_s, o_f, zs, zf, p_prompt[0].reshape(b * s, -1))

    xs = x_sample.reshape(bd * t_new, d_model)
    kvs2, zs2, kvf2, zf2, lf2, qs2, _, qf2, _ = proj(xs)
    n_pool, page = cache_sb_kv.shape[1], cache_sb_kv.shape[2]
    pool_view = lambda c: jnp.transpose(c[0], (0, 2, 3, 4, 1)).reshape(n_pool, -1, page)
    lft_new = jnp.pad(jnp.transpose(lf2.reshape(bd, t_new, h_fox), (0, 2, 1)),
                      ((0, 0), (0, 0), (0, page - t_new)))
    o_s2, o_f2 = _sample_attend(
        page_table, qs2.astype(F32), qf2.astype(F32), kvs2, kvf2, lft_new,
        pool_view(cache_sb_kv), pool_view(cache_fox_kv), jnp.transpose(cache_fox_logf[0], (0, 2, 1)),
        t_new=t_new, n_heads=h_sb)
    y_sample = merge(xs, o_s2, o_f2, zs2, zf2, p_sample[0].reshape(bd * t_new, -1))

    new_kv = lambda t, n_heads: jnp.transpose(t.reshape(b, 2, n_heads, HEAD_DIM, s), (0, 4, 1, 2, 3))[None]
    return (y_prompt.reshape(b, s, d_model), y_sample.reshape(bd, t_new, d_model),
            new_kv(kvs_t, h_sb), new_kv(kvf_t, h_fox),
            lf.reshape(1, b, s, h_fox),
            kvs2.reshape(1, bd, t_new, 2, h_sb, HEAD_DIM), kvf2.reshape(1, bd, t_new, 2, h_fox, HEAD_DIM),
            lf2.reshape(1, bd, t_new, h_fox))
```
